```python
import functools
import jax, jax.numpy as jnp
from jax import lax
import numpy as np

D_MODEL = 1024
BATCH = 8
SEQ = 8192
DEPTH = 1
DEC_BATCH = 128
DEC_SEQ = 4
PAST_LEN = 8192
PAGE_SIZE = 128

MIX_WIDTH = D_MODEL
ATTN_WIDTH = MIX_WIDTH // 2
CONV_WIDTH = MIX_WIDTH - ATTN_WIDTH
HEAD_DIM = 64
N_HEADS = ATTN_WIDTH // HEAD_DIM
ROT_DIM = HEAD_DIM // 4
ROPE_THETA = 500000.0
MOBA_BLOCK = 256
MOBA_TOPK = 3
Q_BLOCK = 128
CONV_K = 31
D_FF = -(-8 * D_MODEL // (3 * 256)) * 256
IN_COLS = 3 * ATTN_WIDTH + 2 * CONV_WIDTH
NORM_EPS = 1e-6

kernel_name = "hymba_moba_conformer_step"


def rms_norm(x, g):
    xf = x.astype(jnp.float32)
    y = xf * lax.rsqrt(jnp.mean(xf * xf, axis=-1, keepdims=True) + NORM_EPS)
    return (y * g.astype(jnp.float32)).astype(x.dtype)


def layer_norm(x, g, b):
    xf = x.astype(jnp.float32)
    mu = jnp.mean(xf, axis=-1, keepdims=True)
    xc = xf - mu
    y = xc * lax.rsqrt(jnp.mean(xc * xc, axis=-1, keepdims=True) + NORM_EPS)
    return (y * g.astype(jnp.float32) + b.astype(jnp.float32)).astype(x.dtype)


def rope_partial(x, pos):
    inv = ROPE_THETA ** (-jnp.arange(0, ROT_DIM, 2, dtype=jnp.float32) / ROT_DIM)
    ang = pos.astype(jnp.float32)[:, None] * inv[None, :]
    cos = jnp.cos(ang)[:, None, :]
    sin = jnp.sin(ang)[:, None, :]
    xf = x.astype(jnp.float32)
    x1 = xf[..., : ROT_DIM // 2]
    x2 = xf[..., ROT_DIM // 2: ROT_DIM]
    out = jnp.concatenate([x1 * cos - x2 * sin, x2 * cos + x1 * sin, xf[..., ROT_DIM:]], axis=-1)
    return out.astype(x.dtype)


def pad_to_block(a):
    L = a.shape[0]
    Lp = -(-L // MOBA_BLOCK) * MOBA_BLOCK
    return jnp.pad(a, ((0, Lp - L), (0, 0), (0, 0)))


def block_views(k, v):
    nb = k.shape[0] // MOBA_BLOCK
    kb = k.reshape(nb, MOBA_BLOCK, N_HEADS, HEAD_DIM)
    vb = v.reshape(nb, MOBA_BLOCK, N_HEADS, HEAD_DIM)
    k_mean = jnp.mean(kb.astype(jnp.float32), axis=1)
    return kb.transpose(2, 0, 1, 3), vb.transpose(2, 0, 1, 3), k_mean


def moba_attend(q, q_pos, k_blocks, v_blocks, k_mean):
    nq = q.shape[0]
    nb = k_blocks.shape[1]
    neg = jnp.finfo(jnp.float32).min
    gate = jnp.einsum('qhd,nhd->qhn', q.astype(jnp.float32), k_mean)
    own = q_pos // MOBA_BLOCK
    fully_past = jnp.arange(nb)[None, :] < own[:, None]
    gate = jnp.where(fully_past[:, None, :], gate, neg)
    if nb < MOBA_TOPK:
        gate = jnp.pad(gate, ((0, 0), (0, 0), (0, MOBA_TOPK - nb)), constant_values=neg)
    _, sel = lax.top_k(gate, MOBA_TOPK)
    sel = jnp.minimum(sel, nb - 1)
    sel_ok = jnp.arange(MOBA_TOPK)[None, :] < jnp.minimum(own, MOBA_TOPK)[:, None]
    idx = jnp.concatenate(
        [sel, jnp.broadcast_to(own[:, None, None], (nq, N_HEADS, 1)).astype(sel.dtype)], axis=-1)
    heads = jnp.arange(N_HEADS)[None, :, None]
    kg = k_blocks[heads, idx]
    vg = v_blocks[heads, idx]
    s = jnp.einsum('qhd,qhnkd->qhnk', q, kg, preferred_element_type=jnp.float32) * (HEAD_DIM ** -0.5)
    key_pos = idx[..., None] * MOBA_BLOCK + jnp.arange(MOBA_BLOCK)
    causal = key_pos <= q_pos[:, None, None, None]
    blk_ok = jnp.concatenate([sel_ok, jnp.ones((nq, 1), dtype=bool)], axis=-1)[:, None, :, None]
    s = jnp.where(causal & blk_ok, s, neg)
    p = jax.nn.softmax(s, axis=(-2, -1))
    return jnp.einsum('qhnk,qhnkd->qhd', p.astype(vg.dtype), vg)


def attend_prompt(q, k, v, pos):
    S = q.shape[1]
    nq = S // Q_BLOCK
    qpos = pos.reshape(nq, Q_BLOCK)

    def per_seq(args):
        qs, ks, vs = args
        kbh, vbh, km = block_views(pad_to_block(ks), pad_to_block(vs))
        out = lax.map(lambda a: moba_attend(a[0], a[1], kbh, vbh, km),
                      (qs.reshape(nq, Q_BLOCK, N_HEADS, HEAD_DIM), qpos))
        return out.reshape(S, N_HEADS, HEAD_DIM)

    return lax.map(per_seq, (q, k, v))


def attend_sample(q, k, v, pos, cache_k, cache_v, page_table, layer):
    past = page_table.shape[1] * PAGE_SIZE

    def per_seq(args):
        qs, ks, vs, pt = args
        kp = cache_k[layer, pt].reshape(past, N_HEADS, HEAD_DIM).astype(ks.dtype)
        vp = cache_v[layer, pt].reshape(past, N_HEADS, HEAD_DIM).astype(vs.dtype)
        kbh, vbh, km = block_views(pad_to_block(jnp.concatenate([kp, ks], axis=0)),
                                   pad_to_block(jnp.concatenate([vp, vs], axis=0)))
        return moba_attend(qs, pos, kbh, vbh, km)

    return lax.map(per_seq, (q, k, v, page_table))


def conv_module(ga, gb, hist, w_dw, b_dw, g_ln, b_ln):
    u = ga * jax.nn.sigmoid(gb)
    u_ext = jnp.concatenate([hist.astype(u.dtype), u], axis=1)
    y = lax.conv_general_dilated(
        u_ext, w_dw[:, None, :].astype(u.dtype), window_strides=(1,), padding='VALID',
        dimension_numbers=('NWC', 'WIO', 'NWC'), feature_group_count=CONV_WIDTH)
    y = y + b_dw.astype(y.dtype)
    y = jax.nn.silu(layer_norm(y, g_ln, b_ln))
    return y, u_ext[:, -(CONV_K - 1):]


def trunk_layer(x, pos, attn_fn, conv_hist, g_mix, w_in, w_dw, b_dw, g_ln, b_ln,
                w_out, g_ffn, w_gate, w_up, w_down):
    B, T = x.shape[:2]
    h = rms_norm(x, g_mix)
    proj = jnp.einsum('btd,de->bte', h, w_in)
    A, C = ATTN_WIDTH, CONV_WIDTH
    q = rope_partial(proj[..., :A].reshape(B, T, N_HEADS, HEAD_DIM), pos)
    k = rope_partial(proj[..., A:2 * A].reshape(B, T, N_HEADS, HEAD_DIM), pos)
    v = proj[..., 2 * A:3 * A].reshape(B, T, N_HEADS, HEAD_DIM)
    ga = proj[..., 3 * A:3 * A + C]
    gb = proj[..., 3 * A + C:]
    a = attn_fn(q, k, v, pos).reshape(B, T, A)
    c, conv_tail = conv_module(ga, gb, conv_hist, w_dw, b_dw, g_ln, b_ln)
    x = x + jnp.einsum('bte,ed->btd', jnp.concatenate([a, c], axis=-1), w_out)
    h2 = rms_norm(x, g_ffn)
    ff = jax.nn.silu(jnp.einsum('btd,df->btf', h2, w_gate)) * jnp.einsum('btd,df->btf', h2, w_up)
    x = x + jnp.einsum('btf,fd->btd', ff, w_down)
    return x, k, v, conv_tail


def setup_inputs(seed: int = 0) -> dict:
    key = jax.random.key(seed)
    ks = jax.random.split(key, 20)
    n_pages = PAST_LEN // PAGE_SIZE
    n_used = DEC_BATCH * n_pages
    n_phys = (5 * n_used + 3) // 4
    f32 = jnp.float32
    x_prompt = jax.random.normal(ks[0], (BATCH, SEQ, D_MODEL), f32)
    x_sample = jax.random.normal(ks[1], (DEC_BATCH, DEC_SEQ, D_MODEL), f32)
    cache_k = jax.random.normal(ks[2], (DEPTH, n_phys, PAGE_SIZE, N_HEADS, HEAD_DIM), f32)
    cache_v = jax.random.normal(ks[3], (DEPTH, n_phys, PAGE_SIZE, N_HEADS, HEAD_DIM), f32)
    state_conv = 0.5 * jax.random.normal(ks[4], (DEPTH, DEC_BATCH, CONV_K - 1, CONV_WIDTH), f32)
    page_table = jax.random.permutation(ks[5], n_phys)[:n_used].reshape(DEC_BATCH, n_pages).astype(jnp.int32)
    g_mix_norm = 1.0 + 0.02 * jax.random.normal(ks[6], (DEPTH, D_MODEL), f32)
    w_in = jax.random.normal(ks[7], (DEPTH, D_MODEL, IN_COLS), f32) * D_MODEL ** -0.5
    w_dw = jax.random.normal(ks[8], (DEPTH, CONV_K, CONV_WIDTH), f32) * CONV_K ** -0.5
    b_dw = 0.02 * jax.random.normal(ks[9], (DEPTH, CONV_WIDTH), f32)
    g_conv_ln = 1.0 + 0.02 * jax.random.normal(ks[10], (DEPTH, CONV_WIDTH), f32)
    b_conv_ln = 0.02 * jax.random.normal(ks[11], (DEPTH, CONV_WIDTH), f32)
    w_out = jax.random.normal(ks[12], (DEPTH, MIX_WIDTH, D_MODEL), f32) * MIX_WIDTH ** -0.5
    g_ffn_norm = 1.0 + 0.02 * jax.random.normal(ks[13], (DEPTH, D_MODEL), f32)
    w_gate = jax.random.normal(ks[14], (DEPTH, D_MODEL, D_FF), f32) * D_MODEL ** -0.5
    w_up = jax.random.normal(ks[15], (DEPTH, D_MODEL, D_FF), f32) * D_MODEL ** -0.5
    w_down = jax.random.normal(ks[16], (DEPTH, D_FF, D_MODEL), f32) * D_FF ** -0.5
    g_final = 1.0 + 0.02 * jax.random.normal(ks[17], (D_MODEL,), f32)
    return {"x_prompt": x_prompt, "x_sample": x_sample, "cache_k": cache_k, "cache_v": cache_v,
            "state_conv": state_conv, "page_table": page_table, "g_mix_norm": g_mix_norm,
            "w_in": w_in, "w_dw": w_dw, "b_dw": b_dw, "g_conv_ln": g_conv_ln,
            "b_conv_ln": b_conv_ln, "w_out": w_out, "g_ffn_norm": g_ffn_norm,
            "w_gate": w_gate, "w_up": w_up, "w_down": w_down, "g_final": g_final}


def reference(x_prompt, x_sample, cache_k, cache_v, state_conv, page_table, g_mix_norm,
              w_in, w_dw, b_dw, g_conv_ln, b_conv_ln, w_out, g_ffn_norm, w_gate, w_up,
              w_down, g_final):
    S = x_prompt.shape[1]
    past = page_table.shape[1] * PAGE_SIZE
    pos_p = jnp.arange(S, dtype=jnp.int32)
    pos_s = past + jnp.arange(x_sample.shape[1], dtype=jnp.int32)
    zero_hist = jnp.zeros((x_prompt.shape[0], CONV_K - 1, CONV_WIDTH), x_prompt.dtype)
    yp, ys = x_prompt, x_sample
    kp_l, vp_l, cp_l, ks_l, vs_l, cs_l = [], [], [], [], [], []
    for l in range(DEPTH):
        lw = (g_mix_norm[l], w_in[l], w_dw[l], b_dw[l], g_conv_ln[l], b_conv_ln[l],
              w_out[l], g_ffn_norm[l], w_gate[l], w_up[l], w_down[l])
        yp, kp, vp, cp = trunk_layer(yp, pos_p, attend_prompt, zero_hist, *lw)
        attn_s = functools.partial(attend_sample, cache_k=cache_k, cache_v=cache_v,
                                   page_table=page_table, layer=l)
        ys, kn, vn, cn = trunk_layer(ys, pos_s, attn_s, state_conv[l], *lw)
        kp_l.append(kp); vp_l.append(vp); cp_l.append(cp)
        ks_l.append(kn); vs_l.append(vn); cs_l.append(cn)
    y_prompt = rms_norm(yp, g_final)
    y_sample = rms_norm(ys, g_final)
    k_prompt = jnp.stack(kp_l); v_prompt = jnp.stack(vp_l); conv_prompt = jnp.stack(cp_l)
    k_sample = jnp.stack(ks_l); v_sample = jnp.stack(vs_l); conv_sample = jnp.stack(cs_l)
    return (y_prompt, y_sample, k_prompt, v_prompt, conv_prompt, k_sample, v_sample, conv_sample)
```

```python
import functools

import jax
import jax.numpy as jnp
from jax import lax
from jax.experimental import pallas as pl
from jax.experimental.pallas import tpu as pltpu

F32 = jnp.float32
BF16 = jnp.bfloat16

D_MODEL = 1024
ATTN_WIDTH = 512
CONV_WIDTH = 512
HEAD_DIM = 64
N_HEADS = ATTN_WIDTH // HEAD_DIM
ROT_DIM = HEAD_DIM // 4
ROPE_THETA = 500000.0
MOBA_BLOCK = 256
MOBA_TOPK = 3
CONV_K = 31
PAGE_SIZE = 128
NORM_EPS = 1e-6
IN_COLS = 3 * ATTN_WIDTH + 2 * CONV_WIDTH

LANES = 128
HEADS_PER_SLAB = LANES // HEAD_DIM
MASKED = -1e30
VMEM_LIMIT = 56 * 1024 * 1024

ROW_TILE = 512
FFN_ROW_TILE = 256
SAMPLE_PAGES_PER_CHUNK = 16
SAMPLE_CONV_SEQS = 8


def _params(*sem):
    return pltpu.CompilerParams(dimension_semantics=sem, vmem_limit_bytes=VMEM_LIMIT)


def _rms(x, g):
    return x * lax.rsqrt(jnp.mean(x * x, axis=-1, keepdims=True) + NORM_EPS) * g


def _rope_tables(pos):
    inv = ROPE_THETA ** (-jnp.arange(0, ROT_DIM, 2, dtype=F32) / ROT_DIM)
    ang = pos.astype(F32)[:, None] * inv[None, :]
    cos, sin = jnp.cos(ang), jnp.sin(ang)
    t, half = pos.shape[0], ROT_DIM // 2
    rest = HEAD_DIM - ROT_DIM
    c = jnp.concatenate([cos, cos, jnp.ones((t, rest), F32)], axis=1)
    lo = jnp.concatenate([-sin, jnp.zeros((t, half + rest), F32)], axis=1)
    hi = jnp.concatenate([jnp.zeros((t, half), F32), sin, jnp.zeros((t, rest), F32)], axis=1)
    return tuple(jnp.tile(a, (1, HEADS_PER_SLAB)) for a in (c, lo, hi))


def _in_proj_kernel(x_ref, g_ref, w_ref, cos_ref, lo_ref, hi_ref, q_ref, k_ref, v_ref, u_ref):
    h = _rms(x_ref[...], g_ref[...])
    proj = jnp.dot(h.astype(BF16), w_ref[...], preferred_element_type=F32)
    cos, lo, hi = cos_ref[...], lo_ref[...], hi_ref[...]
    half = ROT_DIM // 2

    def rope(xs):
        return xs * cos + pltpu.roll(xs, LANES - half, 1) * lo + pltpu.roll(xs, half, 1) * hi

    for s in range(ATTN_WIDTH // LANES):
        sl = slice(s * LANES, (s + 1) * LANES)
        q_ref[:, sl] = rope(proj[:, s * LANES:(s + 1) * LANES])
        k_ref[:, sl] = rope(proj[:, ATTN_WIDTH + s * LANES:ATTN_WIDTH + (s + 1) * LANES])
    a = ATTN_WIDTH
    v_ref[...] = proj[:, 2 * a:3 * a]
    ga = proj[:, 3 * a:3 * a + CONV_WIDTH]
    gb = proj[:, 3 * a + CONV_WIDTH:]
    u_ref[...] = ga * jax.nn.sigmoid(gb)


def _in_proj(x2d, g, w_bf16, tables, tm):
    n = x2d.shape[0]
    table_tiles = tables[0].shape[0] // tm
    tab_spec = pl.BlockSpec((tm, LANES), lambda i: (i % table_tiles, 0))
    out_spec = pl.BlockSpec((tm, ATTN_WIDTH), lambda i: (i, 0))
    out_sds = jax.ShapeDtypeStruct((n, ATTN_WIDTH), F32)
    return pl.pallas_call(
        _in_proj_kernel,
        grid=(n // tm,),
        in_specs=[pl.BlockSpec((tm, D_MODEL), lambda i: (i, 0)),
                  pl.BlockSpec((1, D_MODEL), lambda i: (0, 0)),
                  pl.BlockSpec((D_MODEL, IN_COLS), lambda i: (0, 0)),
                  tab_spec, tab_spec, tab_spec],
        out_specs=[out_spec] * 4,
        out_shape=[out_sds] * 4,
        compiler_params=_params("arbitrary"),
        name="in_proj",
    )(x2d, g, w_bf16, *tables)


def _topk_bias(gate, valid, rown, n_rows):
    neg = jnp.finfo(F32).min
    g = jnp.where(valid, gate, neg)
    sel = jnp.zeros(gate.shape, jnp.bool_)
    for _ in range(min(MOBA_TOPK, n_rows)):
        m = jnp.max(g, axis=0, keepdims=True)
        first = jnp.min(jnp.where(g == m, rown, n_rows), axis=0, keepdims=True)
        pick = rown == first
        sel = sel | pick
        g = jnp.where(pick, -jnp.inf, g)
    return jnp.where(sel & valid, 0.0, MASKED)


def _attn_prompt_kernel(q_ref, k_ref, v_ref, o_ref, kb_ref, vt_ref, km_ref, bias_ref, *, nb):
    i = pl.program_id(2)
    blk = MOBA_BLOCK

    @pl.when(i == 0)
    def _():
        def stage(j, c):
            rows = pl.ds(pl.multiple_of(j * blk, blk), blk)
            kblk = k_ref[0, rows, :]
            kb_ref[j] = kblk.astype(BF16)
            km_ref[pl.ds(j, 1), :] = jnp.mean(kblk, axis=0, keepdims=True)
            vt_ref[j] = v_ref[0, rows, :].T.astype(BF16)
            return c
        lax.fori_loop(0, nb, stage, 0)

    q = q_ref[0]
    lane = lax.broadcasted_iota(jnp.int32, q.shape, 1)
    rown = lax.broadcasted_iota(jnp.int32, (nb, blk), 0)
    key_i = lax.broadcasted_iota(jnp.int32, (blk, blk), 0)
    qry_i = lax.broadcasted_iota(jnp.int32, (blk, blk), 1)
    nt = (((1,), (1,)), ((), ()))
    k_own, vt_own = kb_ref[i], vt_ref[i]
    outs = []
    for h in range(HEADS_PER_SLAB):
        qh = jnp.where((lane >= h * HEAD_DIM) & (lane < (h + 1) * HEAD_DIM), q, 0.0)
        gate = lax.dot_general(km_ref[...], qh, nt, precision=lax.Precision.HIGHEST,
                               preferred_element_type=F32)
        bias_ref[...] = _topk_bias(gate, rown < i, rown, nb)
        qs = (qh * HEAD_DIM ** -0.5).astype(BF16)

        s = lax.dot_general(k_own, qs, nt, preferred_element_type=F32)
        s = jnp.where(key_i <= qry_i, s, MASKED)
        m = jnp.max(s, axis=0, keepdims=True)
        p = jnp.exp(s - m)
        l = jnp.sum(p, axis=0, keepdims=True)
        acc = jnp.dot(vt_own, p.astype(BF16), preferred_element_type=F32)

        def sweep(j, carry):
            m, l, acc = carry
            s = lax.dot_general(kb_ref[j], qs, nt, preferred_element_type=F32)
            s = s + bias_ref[pl.ds(j, 1), :]
            m_new = jnp.maximum(m, jnp.max(s, axis=0, keepdims=True))
            alpha = jnp.exp(m - m_new)
            p = jnp.exp(s - m_new)
            l = alpha * l + jnp.sum(p, axis=0, keepdims=True)
            acc = alpha * acc + jnp.dot(vt_ref[j], p.astype(BF16), preferred_element_type=F32)
            return m_new, l, acc

        m, l, acc = lax.fori_loop(0, i, sweep, (m, l, acc))
        outs.append(acc * (1.0 / l))
    drow = lax.broadcasted_iota(jnp.int32, outs[0].shape, 0)
    out_t = outs[-1]
    for h in range(HEADS_PER_SLAB - 2, -1, -1):
        out_t = jnp.where(drow < (h + 1) * HEAD_DIM, outs[h], out_t)
    o_ref[0] = out_t.T.astype(o_ref.dtype)


def _attn_prompt(q, k, v):
    b, s, _ = q.shape
    nb = s // MOBA_BLOCK
    slabs = ATTN_WIDTH // LANES
    kv_spec = pl.BlockSpec((1, s, LANES), lambda bi, hp, i: (bi, 0, hp))
    q_spec = pl.BlockSpec((1, MOBA_BLOCK, LANES), lambda bi, hp, i: (bi, i, hp))
    return pl.pallas_call(
        functools.partial(_attn_prompt_kernel, nb=nb),
        grid=(b, slabs, nb),
        in_specs=[q_spec, kv_spec, kv_spec],
        out_specs=q_spec,
        out_shape=jax.ShapeDtypeStruct((b, s, ATTN_WIDTH), BF16),
        scratch_shapes=[pltpu.VMEM((nb, MOBA_BLOCK, LANES), BF16),
                        pltpu.VMEM((nb, LANES, MOBA_BLOCK), BF16),
                        pltpu.VMEM((nb, LANES), F32),
                        pltpu.VMEM((nb, MOBA_BLOCK), F32)],
        compiler_params=_params("arbitrary", "arbitrary", "arbitrary"),
        name="attn_prompt",
    )(q, k, v)


def _attn_sample_kernel(pt_ref, q_ref, kn_ref, vn_ref, ck_ref, cv_ref, o_ref,
                        buf_ref, s_ref, gate_ref, own_ref, sem, *, n_pages, ppc, nseq, t_new):
    b = pl.program_id(0)
    blk = MOBA_BLOCK
    nck = n_pages // ppc
    rows = ppc * PAGE_SIZE
    bpc = rows // blk
    nbk = n_pages * PAGE_SIZE // blk
    nchunks = 2 * nck
    nt = (((1,), (1,)), ((), ()))
    tn = (((0,), (0,)), ((), ()))

    def copies(bb, c, slot):
        src = ck_ref if c < nck else cv_ref
        base = (c % nck) * ppc
        return [pltpu.make_async_copy(src.at[pt_ref[bb, base + p]],
                                      buf_ref.at[slot, pl.ds(p * PAGE_SIZE, PAGE_SIZE), :],
                                      sem.at[slot]) for p in range(ppc)]

    @pl.when(b == 0)
    def _():
        for cp in copies(b, 0, 0):
            cp.start()

    q = q_ref[0]
    col_head = lax.broadcasted_iota(jnp.int32, (N_HEADS, ATTN_WIDTH), 1) // HEAD_DIM
    row_head = lax.broadcasted_iota(jnp.int32, (N_HEADS, ATTN_WIDTH), 0)
    headmask = col_head == row_head
    qrows = [jnp.where(headmask, q[t:t + 1, :] * HEAD_DIM ** -0.5, 0.0) for t in range(t_new)]
    qrows.append(jnp.zeros((LANES - t_new * N_HEADS, ATTN_WIDTH), F32))
    qexp = jnp.concatenate(qrows, axis=0).astype(BF16)

    acc = jnp.zeros((LANES, ATTN_WIDTH), F32)
    inv = None
    for c in range(nchunks):
        slot = c % 2
        for cp in copies(b, c, slot):
            cp.wait()
        if c + 1 < nchunks:
            for cp in copies(b, c + 1, 1 - slot):
                cp.start()
        else:
            @pl.when(b + 1 < nseq)
            def _():
                for cp in copies(b + 1, 0, 1 - slot):
                    cp.start()

        if c < nck:
            st = lax.dot_general(buf_ref[slot].astype(BF16), qexp, nt,
                                 preferred_element_type=F32)
            s_ref[pl.ds(c * rows, rows), :] = st
            gate_ref[pl.ds(c * bpc, bpc), :] = jnp.mean(st.reshape(bpc, blk, LANES), axis=1)
        if c == nck - 1:
            rown = lax.broadcasted_iota(jnp.int32, (nbk, LANES), 0)
            gate_ref[...] = _topk_bias(gate_ref[...], rown >= 0, rown, nbk)
            own_ref[...] = jnp.zeros(own_ref.shape, F32)
            own_ref[0:t_new, :] = kn_ref[0]
            sn = lax.dot_general(own_ref[...].astype(BF16), qexp, nt, preferred_element_type=F32)
            key_t = lax.broadcasted_iota(jnp.int32, (blk, LANES), 0)
            qry_t = lax.broadcasted_iota(jnp.int32, (blk, LANES), 1) // N_HEADS
            sn = jnp.where((key_t <= qry_t) & (key_t < t_new), sn, MASKED)
            m = jnp.max(sn, axis=0, keepdims=True)

            def blk_rows(n):
                return pl.ds(pl.multiple_of(n * blk, blk), blk)

            def run_max(n, m):
                s = s_ref[blk_rows(n), :] + gate_ref[pl.ds(n, 1), :]
                return jnp.maximum(m, jnp.max(s, axis=0, keepdims=True))
            m = lax.fori_loop(0, nbk, run_max, m)
            pn = jnp.exp(sn - m)

            def run_exp(n, l):
                p = jnp.exp(s_ref[blk_rows(n), :] + gate_ref[pl.ds(n, 1), :] - m)
                s_ref[blk_rows(n), :] = p
                return l + jnp.sum(p, axis=0, keepdims=True)
            l = lax.fori_loop(0, nbk, run_exp, jnp.sum(pn, axis=0, keepdims=True))
            inv = 1.0 / l
            own_ref[0:t_new, :] = vn_ref[0]
            acc = lax.dot_general((pn * inv).astype(BF16), own_ref[...].astype(BF16), tn,
                                  preferred_element_type=F32)
        if c >= nck:
            p = (s_ref[pl.ds((c - nck) * rows, rows), :] * inv).astype(BF16)
            acc = acc + lax.dot_general(p, buf_ref[slot].astype(BF16), tn,
                                        preferred_element_type=F32)
    outs = [jnp.sum(jnp.where(headmask, acc[t * N_HEADS:(t + 1) * N_HEADS, :], 0.0),
                    axis=0, keepdims=True) for t in range(t_new)]
    o_ref[0] = jnp.concatenate(outs, axis=0).astype(o_ref.dtype)


def _attn_sample(q, k_new, v_new, cache_k, cache_v, page_table):
    nseq, t_new, _ = q.shape
    n_pages = page_table.shape[1]
    ppc = min(SAMPLE_PAGES_PER_CHUNK, n_pages)
    past = n_pages * PAGE_SIZE
    assert past % MOBA_BLOCK == 0 and n_pages % ppc == 0 and (ppc * PAGE_SIZE) % MOBA_BLOCK == 0
    assert t_new * N_HEADS <= LANES and t_new <= MOBA_BLOCK
    tok_spec = pl.BlockSpec((1, t_new, ATTN_WIDTH), lambda b, pt: (b, 0, 0))
    any_spec = pl.BlockSpec(memory_space=pl.ANY)
    rows = ppc * PAGE_SIZE
    return pl.pallas_call(
        functools.partial(_attn_sample_kernel, n_pages=n_pages, ppc=ppc, nseq=nseq, t_new=t_new),
        grid_spec=pltpu.PrefetchScalarGridSpec(
            num_scalar_prefetch=1,
            grid=(nseq,),
            in_specs=[tok_spec, tok_spec, tok_spec, any_spec, any_spec],
            out_specs=tok_spec,
            scratch_shapes=[pltpu.VMEM((2, rows, ATTN_WIDTH), F32),
                            pltpu.VMEM((past, LANES), F32),
                            pltpu.VMEM((past // MOBA_BLOCK, LANES), F32),
                            pltpu.VMEM((MOBA_BLOCK, ATTN_WIDTH), F32),
                            pltpu.SemaphoreType.DMA((2,))]),
        out_shape=jax.ShapeDtypeStruct((nseq, t_new, ATTN_WIDTH), BF16),
        compiler_params=_params("arbitrary"),
        name="attn_sample",
    )(page_table, q, k_new, v_new, cache_k, cache_v)


def _ln_swish(y, g, b):
    mu = jnp.mean(y, axis=-1, keepdims=True)
    yc = y - mu
    yn = yc * lax.rsqrt(jnp.mean(yc * yc, axis=-1, keepdims=True) + NORM_EPS) * g + b
    return yn * jax.nn.sigmoid(yn)


CONV_HALO = 32
CONV_CHUNK = 64


def _conv_prompt_kernel(u_ref, prev_ref, w_ref, bd_ref, g_ref, bl_ref, c_ref, tail_ref, ext_ref,
                        *, tc, last):
    i = pl.program_id(1)
    hist = CONV_K - 1
    ext_ref[0:CONV_HALO, :] = jnp.where(i > 0, prev_ref[0], 0.0)
    ext_ref[CONV_HALO:CONV_HALO + tc, :] = u_ref[0]
    first = CONV_HALO - hist
    for r in range(tc // CONV_CHUNK):
        acc = jnp.zeros((CONV_CHUNK, CONV_WIDTH), F32)
        for k in range(CONV_K):
            win = ext_ref[first + r * CONV_CHUNK + k:first + (r + 1) * CONV_CHUNK + k, :]
            acc = acc + win * w_ref[k:k + 1, :]
        y = _ln_swish(acc + bd_ref[...], g_ref[...], bl_ref[...])
        c_ref[0, r * CONV_CHUNK:(r + 1) * CONV_CHUNK, :] = y.astype(c_ref.dtype)

    @pl.when(i == last)
    def _():
        tail_ref[0] = ext_ref[CONV_HALO + tc - hist:CONV_HALO + tc, :]


def _conv_prompt(u, w_dw, b_dw, g_ln, b_ln, tc):
    b, s, _ = u.shape
    nt = s // tc
    halo_per_tile = tc // CONV_HALO
    vec = pl.BlockSpec((1, CONV_WIDTH), lambda bi, i: (0, 0))
    return pl.pallas_call(
        functools.partial(_conv_prompt_kernel, tc=tc, last=nt - 1),
        grid=(b, nt),
        in_specs=[pl.BlockSpec((1, tc, CONV_WIDTH), lambda bi, i: (bi, i, 0)),
                  pl.BlockSpec((1, CONV_HALO, CONV_WIDTH),
                               lambda bi, i: (bi, jnp.maximum(i * halo_per_tile - 1, 0), 0)),
                  pl.BlockSpec((CONV_K, CONV_WIDTH), lambda bi, i: (0, 0)),
                  vec, vec, vec],
        out_specs=[pl.BlockSpec((1, tc, CONV_WIDTH), lambda bi, i: (bi, i, 0)),
                   pl.BlockSpec((1, CONV_K - 1, CONV_WIDTH), lambda bi, i: (bi, 0, 0))],
        out_shape=[jax.ShapeDtypeStruct((b, s, CONV_WIDTH), BF16),
                   jax.ShapeDtypeStruct((b, CONV_K - 1, CONV_WIDTH), F32)],
        scratch_shapes=[pltpu.VMEM((CONV_HALO + tc, CONV_WIDTH), F32)],
        compiler_params=_params("arbitrary", "arbitrary"),
        name="conv_prompt",
    )(u, u, w_dw, b_dw, g_ln, b_ln)


def _conv_sample_kernel(u_ref, hist_ref, w_ref, bd_ref, g_ref, bl_ref, c_ref, tail_ref, ext_ref,
                        *, t_new):
    hist = CONV_K - 1
    ext_ref[:, 0:hist, :] = hist_ref[...]
    ext_ref[:, hist:hist + t_new, :] = u_ref[...]
    w = w_ref[...]
    for t in range(t_new):
        y = jnp.sum(ext_ref[:, t:t + CONV_K, :] * w[None], axis=1) + bd_ref[...]
        c_ref[:, t, :] = _ln_swish(y, g_ref[...], bl_ref[...]).astype(c_ref.dtype)
    tail_ref[...] = ext_ref[:, t_new:t_new + hist, :]


def _conv_sample(u, hist, w_dw, b_dw, g_ln, b_ln):
    nseq, t_new, _ = u.shape
    ns = min(SAMPLE_CONV_SEQS, nseq)
    vec = pl.BlockSpec((1, CONV_WIDTH), lambda i: (0, 0))
    tok = pl.BlockSpec((ns, t_new, CONV_WIDTH), lambda i: (i, 0, 0))
    his = pl.BlockSpec((ns, CONV_K - 1, CONV_WIDTH), lambda i: (i, 0, 0))
    return pl.pallas_call(
        functools.partial(_conv_sample_kernel, t_new=t_new),
        grid=(nseq // ns,),
        in_specs=[tok, his, pl.BlockSpec((CONV_K, CONV_WIDTH), lambda i: (0, 0)), vec, vec, vec],
        out_specs=[tok, his],
        out_shape=[jax.ShapeDtypeStruct((nseq, t_new, CONV_WIDTH), BF16),
                   jax.ShapeDtypeStruct((nseq, CONV_K - 1, CONV_WIDTH), F32)],
        scratch_shapes=[pltpu.VMEM((ns, CONV_K - 1 + t_new, CONV_WIDTH), F32)],
        compiler_params=_params("arbitrary"),
        name="conv_sample",
    )(u, hist, w_dw, b_dw, g_ln, b_ln)


def _ffn_kernel(x_ref, a_ref, c_ref, woa_ref, woc_ref, g2_ref, wg_ref, wu_ref, wd_ref, gf_ref,
                y_ref, *, final_norm):
    mix = jnp.dot(a_ref[...], woa_ref[...], preferred_element_type=F32)
    mix = mix + jnp.dot(c_ref[...], woc_ref[...], preferred_element_type=F32)
    x1 = x_ref[...] + mix
    h2 = _rms(x1, g2_ref[...]).astype(BF16)
    gate = jnp.dot(h2, wg_ref[...], preferred_element_type=F32)
    up = jnp.dot(h2, wu_ref[...], preferred_element_type=F32)
    ff = (gate * jax.nn.sigmoid(gate) * up).astype(BF16)
    x2 = x1 + jnp.dot(ff, wd_ref[...], preferred_element_type=F32)
    y_ref[...] = _rms(x2, gf_ref[...]) if final_norm else x2


def _ffn(x2d, a2d, c2d, woa, woc, g2, wg, wu, wd, gf, tm, final_norm):
    n = x2d.shape[0]
    d_ff = wg.shape[1]

    def resident(shape):
        return pl.BlockSpec(shape, lambda i: (0, 0), pipeline_mode=pl.Buffered(1))

    return pl.pallas_call(
        functools.partial(_ffn_kernel, final_norm=final_norm),
        grid=(n // tm,),
        in_specs=[pl.BlockSpec((tm, D_MODEL), lambda i: (i, 0)),
                  pl.BlockSpec((tm, ATTN_WIDTH), lambda i: (i, 0)),
                  pl.BlockSpec((tm, CONV_WIDTH), lambda i: (i, 0)),
                  resident((ATTN_WIDTH, D_MODEL)), resident((CONV_WIDTH, D_MODEL)),
                  resident((1, D_MODEL)),
                  resident((D_MODEL, d_ff)), resident((D_MODEL, d_ff)), resident((d_ff, D_MODEL)),
                  resident((1, D_MODEL))],
        out_specs=pl.BlockSpec((tm, D_MODEL), lambda i: (i, 0)),
        out_shape=jax.ShapeDtypeStruct((n, D_MODEL), F32),
        compiler_params=_params("arbitrary"),
        name="out_proj_ffn",
    )(x2d, a2d, c2d, woa, woc, g2, wg, wu, wd, gf)


def kernel(x_prompt, x_sample, cache_k, cache_v, state_conv, page_table, g_mix_norm, w_in, w_dw,
           b_dw, g_conv_ln, b_conv_ln, w_out, g_ffn_norm, w_gate, w_up, w_down, g_final):
    depth = w_in.shape[0]
    bsz, seq, _ = x_prompt.shape
    nseq, t_new, _ = x_sample.shape
    n_phys = cache_k.shape[1]
    past = page_table.shape[1] * PAGE_SIZE
    assert seq % MOBA_BLOCK == 0 and seq % ROW_TILE == 0

    tabs_p = _rope_tables(jnp.arange(seq, dtype=jnp.int32))
    pos_s = past + jnp.arange(t_new, dtype=jnp.int32)
    tabs_s = _rope_tables(jnp.tile(pos_s, nseq))
    n_p, n_s = bsz * seq, nseq * t_new
    tm_s = min(ROW_TILE, n_s)
    tf_s = min(FFN_ROW_TILE, n_s)

    yp = x_prompt.reshape(n_p, D_MODEL)
    ys = x_sample.reshape(n_s, D_MODEL)
    row = lambda a: a.reshape(1, -1)
    outs = [[] for _ in range(6)]
    for l in range(depth):
        last = l == depth - 1
        w_in_b = w_in[l].astype(BF16)
        woa, woc = w_out[l, :ATTN_WIDTH].astype(BF16), w_out[l, ATTN_WIDTH:].astype(BF16)
        wg, wu, wd = w_gate[l].astype(BF16), w_up[l].astype(BF16), w_down[l].astype(BF16)
        conv_w = (w_dw[l], row(b_dw[l]), row(g_conv_ln[l]), row(b_conv_ln[l]))
        ffn_w = (woa, woc, row(g_ffn_norm[l]), wg, wu, wd, row(g_final))

        q, k, v, u = _in_proj(yp, row(g_mix_norm[l]), w_in_b, tabs_p, ROW_TILE)
        shp = (bsz, seq, ATTN_WIDTH)
        a = _attn_prompt(q.reshape(shp), k.reshape(shp), v.reshape(shp))
        c, tail = _conv_prompt(u.reshape(bsz, seq, CONV_WIDTH), *conv_w, ROW_TILE)
        yp = _ffn(yp, a.reshape(n_p, ATTN_WIDTH), c.reshape(n_p, CONV_WIDTH), *ffn_w,
                  FFN_ROW_TILE, last)
        outs[0].append(k.reshape(bsz, seq, N_HEADS, HEAD_DIM))
        outs[1].append(v.reshape(bsz, seq, N_HEADS, HEAD_DIM))
        outs[2].append(tail)

        q, k, v, u = _in_proj(ys, row(g_mix_norm[l]), w_in_b, tabs_s, tm_s)
        shp = (nseq, t_new, ATTN_WIDTH)
        a = _attn_sample(q.reshape(shp), k.reshape(shp), v.reshape(shp),
                         cache_k.reshape(depth * n_phys, PAGE_SIZE, ATTN_WIDTH),
                         cache_v.reshape(depth * n_phys, PAGE_SIZE, ATTN_WIDTH),
                         page_table + l * n_phys)
        c, tail = _conv_sample(u.reshape(nseq, t_new, CONV_WIDTH), state_conv[l], *conv_w)
        ys = _ffn(ys, a.reshape(n_s, ATTN_WIDTH), c.reshape(n_s, CONV_WIDTH), *ffn_w, tf_s, last)
        outs[3].append(k.reshape(nseq, t_new, N_HEADS, HEAD_DIM))
        outs[4].append(v.reshape(nseq, t_new, N_HEADS, HEAD_DIM))
        outs[5].append(tail)

    kp, vp, cp, ks, vs, cs = (jnp.stack(o) for o in outs)
    return (yp.reshape(bsz, seq, D_MODEL), ys.reshape(nseq, t_new, D_MODEL), kp, vp, cp, ks, vs, cs)
```

```python
import functools

import jax
import jax.numpy as jnp
from jax import lax
from jax.experimental import pallas as pl
from jax.experimental.pallas import tpu as pltpu

F32 = jnp.float32
BF16 = jnp.bfloat16

D_MODEL = 1024
ATTN_WIDTH = 512
CONV_WIDTH = 512
HEAD_DIM = 64
N_HEADS = ATTN_WIDTH // HEAD_DIM
ROT_DIM = HEAD_DIM // 4
ROPE_THETA = 500000.0
MOBA_BLOCK = 256
MOBA_TOPK = 3
CONV_K = 31
PAGE_SIZE = 128
NORM_EPS = 1e-6
IN_COLS = 3 * ATTN_WIDTH + 2 * CONV_WIDTH

LANES = 128
HEADS_PER_SLAB = LANES // HEAD_DIM
MASKED = -1e30
VMEM_LIMIT = 56 * 1024 * 1024

ROW_TILE = 512
FFN_ROW_TILE = 256
SAMPLE_PAGES_PER_CHUNK = 16
SAMPLE_CONV_SEQS = 8


def _params(*sem):
    return pltpu.CompilerParams(dimension_semantics=sem, vmem_limit_bytes=VMEM_LIMIT)


def _rms(x, g):
    return x * lax.rsqrt(jnp.mean(x * x, axis=-1, keepdims=True) + NORM_EPS) * g


def _rope_tables(pos):
    inv = ROPE_THETA ** (-jnp.arange(0, ROT_DIM, 2, dtype=F32) / ROT_DIM)
    ang = pos.astype(F32)[:, None] * inv[None, :]
    cos, sin = jnp.cos(ang), jnp.sin(ang)
    t, half = pos.shape[0], ROT_DIM // 2
    rest = HEAD_DIM - ROT_DIM
    c = jnp.concatenate([cos, cos, jnp.ones((t, rest), F32)], axis=1)
    lo = jnp.concatenate([-sin, jnp.zeros((t, half + rest), F32)], axis=1)
    hi = jnp.concatenate([jnp.zeros((t, half), F32), sin, jnp.zeros((t, rest), F32)], axis=1)
    return tuple(jnp.tile(a, (1, HEADS_PER_SLAB)) for a in (c, lo, hi))


def _in_proj_kernel(x_ref, g_ref, w_ref, cos_ref, lo_ref, hi_ref, q_ref, k_ref, v_ref, u_ref):
    h = _rms(x_ref[...], g_ref[...])
    proj = jnp.dot(h.astype(BF16), w_ref[...], preferred_element_type=F32)
    cos, lo, hi = cos_ref[...], lo_ref[...], hi_ref[...]
    half = ROT_DIM // 2

    def rope(xs):
        return xs * cos + pltpu.roll(xs, LANES - half, 1) * lo + pltpu.roll(xs, half, 1) * hi

    for s in range(ATTN_WIDTH // LANES):
        sl = slice(s * LANES, (s + 1) * LANES)
        q_ref[:, sl] = rope(proj[:, s * LANES:(s + 1) * LANES])
        k_ref[:, sl] = rope(proj[:, ATTN_WIDTH + s * LANES:ATTN_WIDTH + (s + 1) * LANES])
    a = ATTN_WIDTH
    v_ref[...] = proj[:, 2 * a:3 * a]
    ga = proj[:, 3 * a:3 * a + CONV_WIDTH]
    gb = proj[:, 3 * a + CONV_WIDTH:]
    u_ref[...] = ga * jax.nn.sigmoid(gb)


def _in_proj(x2d, g, w_bf16, tables, tm):
    n = x2d.shape[0]
    table_tiles = tables[0].shape[0] // tm
    tab_spec = pl.BlockSpec((tm, LANES), lambda i: (i % table_tiles, 0))
    out_spec = pl.BlockSpec((tm, ATTN_WIDTH), lambda i: (i, 0))
    out_sds = jax.ShapeDtypeStruct((n, ATTN_WIDTH), F32)
    return pl.pallas_call(
        _in_proj_kernel,
        grid=(n // tm,),
        in_specs=[pl.BlockSpec((tm, D_MODEL), lambda i: (i, 0)),
                  pl.BlockSpec((1, D_MODEL), lambda i: (0, 0)),
                  pl.BlockSpec((D_MODEL, IN_COLS), lambda i: (0, 0)),
                  tab_spec, tab_spec, tab_spec],
        out_specs=[out_spec] * 4,
        out_shape=[out_sds] * 4,
        compiler_params=_params("arbitrary"),
        name="in_proj",
    )(x2d, g, w_bf16, *tables)


def _topk_bias(gate, valid, rown, n_rows):
    neg = jnp.finfo(F32).min
    g = jnp.where(valid, gate, neg)
    sel = jnp.zeros(gate.shape, jnp.bool_)
    for _ in range(min(MOBA_TOPK, n_rows)):
        m = jnp.max(g, axis=0, keepdims=True)
        first = jnp.min(jnp.where(g == m, rown, n_rows), axis=0, keepdims=True)
        pick = rown == first
        sel = sel | pick
        g = jnp.where(pick, -jnp.inf, g)
    return jnp.where(sel & valid, 0.0, MASKED)


ATTN_CHAINS = 2
ATTN_CHAIN_BLOCKS = 1
ACC_ROWS = HEAD_DIM + 16
LOG2E = 1.4426950408889634


def _attn_prompt_kernel(q_ref, k_ref, v_ref, o_ref, ka_ref, vt_ref, km_ref, qa_ref, sa_ref, sb_ref,
                        *, nb):
    i = pl.program_id(2)
    blk = MOBA_BLOCK
    grp = ATTN_CHAIN_BLOCKS * blk
    step_blocks = ATTN_CHAINS * ATTN_CHAIN_BLOCKS
    heads = range(HEADS_PER_SLAB)
    ones = jnp.ones((ACC_ROWS - HEAD_DIM, grp), BF16)

    @pl.when(i == 0)
    def _():
        lane = lax.broadcasted_iota(jnp.int32, (blk, LANES), 1)

        def stage_k(j, c):
            rows = pl.ds(pl.multiple_of(j * blk, blk), blk)
            kblk = k_ref[0, rows, :]
            ka_ref[rows, 0:LANES] = kblk.astype(BF16)
            ka_ref[rows, LANES:2 * LANES] = jnp.where(lane == j, 1.0, 0.0).astype(BF16)
            km_ref[pl.ds(j, 1), :] = jnp.mean(kblk, axis=0, keepdims=True)
            return c
        lax.fori_loop(0, nb, stage_k, 0)

        def stage_v(g, c):
            rows = pl.ds(pl.multiple_of(g * grp, grp), grp)
            vt = v_ref[0, rows, :].T.astype(BF16)
            for h in heads:
                vt_ref[g, h, 0:HEAD_DIM, :] = vt[h * HEAD_DIM:(h + 1) * HEAD_DIM, :]
                vt_ref[g, h, HEAD_DIM:ACC_ROWS, :] = ones
            return c
        lax.fori_loop(0, nb // ATTN_CHAIN_BLOCKS, stage_v, 0)

    own = pl.ds(pl.multiple_of(i * blk, blk), blk)
    k_own = k_ref[0, own, :].astype(BF16)
    vt_own = v_ref[0, own, :].T.astype(BF16)
    q_t = q_ref[0].T
    drow = lax.broadcasted_iota(jnp.int32, q_t.shape, 0)
    rown = lax.broadcasted_iota(jnp.int32, (nb, blk), 0)
    causal = (lax.broadcasted_iota(jnp.int32, (blk, blk), 0)
              <= lax.broadcasted_iota(jnp.int32, (blk, blk), 1))
    init = []
    for h in heads:
        qh_t = jnp.where((drow >= h * HEAD_DIM) & (drow < (h + 1) * HEAD_DIM), q_t, 0.0)
        gate = jnp.dot(km_ref[...], qh_t, precision=lax.Precision.HIGHEST,
                       preferred_element_type=F32)
        bias = _topk_bias(gate, rown < i, rown, nb)
        qs_t = (qh_t * (LOG2E * HEAD_DIM ** -0.5)).astype(BF16)
        qa_ref[h, 0:LANES, :] = qs_t
        qa_ref[h, LANES:LANES + nb, :] = bias.astype(BF16)
        if nb < LANES:
            qa_ref[h, LANES + nb:2 * LANES, :] = jnp.zeros((LANES - nb, blk), BF16)

        s = jnp.dot(k_own, qs_t, preferred_element_type=F32)
        s = jnp.where(causal, s, MASKED)
        m = jnp.max(s, axis=0, keepdims=True)
        p = jnp.exp2(s - m)
        vt_ext = jnp.concatenate([vt_own[h * HEAD_DIM:(h + 1) * HEAD_DIM, :], ones[:, :blk]], axis=0)
        acc = jnp.dot(vt_ext, p.astype(BF16), preferred_element_type=F32)
        init.append(((m, acc),) + ((m, jnp.zeros_like(acc)),) * (ATTN_CHAINS - 1))

    def score_step(s_ref, step):
        for c in range(ATTN_CHAINS):
            g = step * ATTN_CHAINS + c
            k_rows = ka_ref[pl.ds(pl.multiple_of(g * grp, grp), grp), :]
            for h in heads:
                s_ref[c, h] = jnp.dot(k_rows, qa_ref[h], preferred_element_type=F32)

    def softmax_step(s_ref, step, state):
        out = []
        for h in heads:
            chains = []
            for c in range(ATTN_CHAINS):
                m, acc = state[h][c]
                s = s_ref[c, h]
                m_new = jnp.maximum(m, jnp.max(s, axis=0, keepdims=True))
                p = jnp.exp2(s - m_new).astype(BF16)
                pv = jnp.dot(vt_ref[step * ATTN_CHAINS + c, h], p, preferred_element_type=F32)
                chains.append((m_new, jnp.exp2(m - m_new) * acc + pv))
            out.append(tuple(chains))
        return tuple(out)

    n_steps_all = nb // step_blocks

    def sweep(u, state):
        score_step(sb_ref, 2 * u + 1)
        state = softmax_step(sa_ref, 2 * u, state)
        score_step(sa_ref, jnp.minimum(2 * u + 2, n_steps_all - 1))
        return softmax_step(sb_ref, 2 * u + 1, state)

    score_step(sa_ref, 0)
    trip_blocks = 2 * step_blocks
    state = lax.fori_loop(0, (i + trip_blocks - 1) // trip_blocks, sweep, tuple(init))
    outs = []
    for h in heads:
        m_all = functools.reduce(jnp.maximum, [m for m, _ in state[h]])
        acc = sum(jnp.exp2(m - m_all) * a for m, a in state[h])
        outs.append(acc[0:HEAD_DIM] * (1.0 / acc[HEAD_DIM:HEAD_DIM + 1]))
    o_ref[0] = jnp.concatenate(outs, axis=0).T.astype(o_ref.dtype)


def _attn_prompt(q, k, v):
    b, s, _ = q.shape
    nb = s // MOBA_BLOCK
    assert nb <= LANES and nb % (2 * ATTN_CHAINS * ATTN_CHAIN_BLOCKS) == 0
    slabs = ATTN_WIDTH // LANES
    grp = ATTN_CHAIN_BLOCKS * MOBA_BLOCK
    kv_spec = pl.BlockSpec((1, s, LANES), lambda bi, hp, i: (bi, 0, hp))
    q_spec = pl.BlockSpec((1, MOBA_BLOCK, LANES), lambda bi, hp, i: (bi, i, hp))
    score_buf = pltpu.VMEM((ATTN_CHAINS, HEADS_PER_SLAB, grp, MOBA_BLOCK), F32)
    return pl.pallas_call(
        functools.partial(_attn_prompt_kernel, nb=nb),
        grid=(b, slabs, nb),
        in_specs=[q_spec, kv_spec, kv_spec],
        out_specs=q_spec,
        out_shape=jax.ShapeDtypeStruct((b, s, ATTN_WIDTH), BF16),
        scratch_shapes=[pltpu.VMEM((s, 2 * LANES), BF16),
                        pltpu.VMEM((nb // ATTN_CHAIN_BLOCKS, HEADS_PER_SLAB, ACC_ROWS, grp), BF16),
                        pltpu.VMEM((nb, LANES), F32),
                        pltpu.VMEM((HEADS_PER_SLAB, 2 * LANES, MOBA_BLOCK), BF16),
                        score_buf, score_buf],
        compiler_params=_params("arbitrary", "arbitrary", "arbitrary"),
        name="attn_prompt",
    )(q, k, v)


def _attn_sample_kernel(pt_ref, q_ref, kn_ref, vn_ref, ck_ref, cv_ref, o_ref,
                        buf_ref, s_ref, p_ref, own_ref, sem, *, n_pages, ppc, nseq, t_new):
    b = pl.program_id(0)
    blk = MOBA_BLOCK
    nck = n_pages // ppc
    toks = ppc * PAGE_SIZE
    nbk = n_pages * PAGE_SIZE // blk
    nchunks = 2 * nck
    nq = t_new * N_HEADS
    nt = (((1,), (1,)), ((), ()))

    def copies(bb, c, slot):
        src = ck_ref if c < nck else cv_ref
        base = (c % nck) * ppc
        return [pltpu.make_async_copy(src.at[pt_ref[bb, base + p]],
                                      buf_ref.at[slot, :, pl.ds(p * PAGE_SIZE, PAGE_SIZE)],
                                      sem.at[slot]) for p in range(ppc)]

    @pl.when(b == 0)
    def _():
        for cp in copies(b, 0, 0):
            cp.start()

    q = q_ref[0]
    col_head = lax.broadcasted_iota(jnp.int32, (N_HEADS, ATTN_WIDTH), 1) // HEAD_DIM
    row_head = lax.broadcasted_iota(jnp.int32, (N_HEADS, ATTN_WIDTH), 0)
    headmask = col_head == row_head
    qexp = jnp.concatenate([jnp.where(headmask, q[t:t + 1, :] * HEAD_DIM ** -0.5, 0.0)
                            for t in range(t_new)], axis=0).astype(BF16)

    acc = m = l = None
    for c in range(nchunks):
        slot = c % 2
        for cp in copies(b, c, slot):
            cp.wait()
        if c + 1 < nchunks:
            for cp in copies(b, c + 1, 1 - slot):
                cp.start()
        else:
            @pl.when(b + 1 < nseq)
            def _():
                for cp in copies(b + 1, 0, 1 - slot):
                    cp.start()

        if c < nck:
            s_ref[:, c * toks:(c + 1) * toks] = jnp.dot(qexp, buf_ref[slot].astype(BF16),
                                                        preferred_element_type=F32)
        if c == nck - 1:
            gate = [jnp.sum(s_ref[:, n * blk:(n + 1) * blk], axis=1, keepdims=True)
                    for n in range(nbk)]
            picked = [jnp.zeros((nq, 1), jnp.bool_)] * nbk
            for _ in range(min(MOBA_TOPK, nbk)):
                top = functools.reduce(jnp.maximum, gate)
                found = jnp.zeros((nq, 1), jnp.bool_)
                for n in range(nbk):
                    hit = (gate[n] == top) & jnp.logical_not(found)
                    found = found | hit
                    picked[n] = picked[n] | hit
                    gate[n] = jnp.where(hit, -jnp.inf, gate[n])
            bias = [jnp.where(pk, 0.0, MASKED) for pk in picked]

            own_ref[...] = jnp.zeros(own_ref.shape, F32)
            own_ref[0:t_new, :] = kn_ref[0]
            sn = lax.dot_general(qexp, own_ref[...].astype(BF16), nt,
                                 preferred_element_type=F32)
            key_t = lax.broadcasted_iota(jnp.int32, sn.shape, 1)
            qry_t = lax.broadcasted_iota(jnp.int32, sn.shape, 0) // N_HEADS
            sn = jnp.where((key_t <= qry_t) & (key_t < t_new), sn, MASKED)
            m = jnp.max(sn, axis=1, keepdims=True)
            for n in range(nbk):
                blk_max = jnp.max(s_ref[:, n * blk:(n + 1) * blk], axis=1, keepdims=True)
                m = jnp.maximum(m, blk_max + bias[n])
            pn = jnp.exp(sn - m)
            l = jnp.sum(pn, axis=1, keepdims=True)
            for n in range(nbk):
                p = jnp.exp(s_ref[:, n * blk:(n + 1) * blk] + (bias[n] - m))
                l = l + jnp.sum(p, axis=1, keepdims=True)
                p_ref[:, n * blk:(n + 1) * blk] = p.astype(BF16)
            own_ref[0:t_new, :] = vn_ref[0]
            acc = jnp.dot(pn.astype(BF16), own_ref[...].astype(BF16),
                          preferred_element_type=F32)
        if c >= nck:
            acc = acc + lax.dot_general(p_ref[:, (c - nck) * toks:(c - nck + 1) * toks],
                                        buf_ref[slot].astype(BF16), nt,
                                        preferred_element_type=F32)
    out = acc * (1.0 / l)
    rows = [jnp.sum(jnp.where(headmask, out[t * N_HEADS:(t + 1) * N_HEADS, :], 0.0),
                    axis=0, keepdims=True) for t in range(t_new)]
    o_ref[0] = jnp.concatenate(rows, axis=0).astype(o_ref.dtype)


def _attn_sample(q, k_new, v_new, cache_kt, cache_vt, page_table):
    nseq, t_new, _ = q.shape
    n_pages = page_table.shape[1]
    ppc = min(SAMPLE_PAGES_PER_CHUNK, n_pages)
    past = n_pages * PAGE_SIZE
    assert past % MOBA_BLOCK == 0 and n_pages % ppc == 0
    assert t_new <= LANES
    tok_spec = pl.BlockSpec((1, t_new, ATTN_WIDTH), lambda b, pt: (b, 0, 0))
    any_spec = pl.BlockSpec(memory_space=pl.ANY)
    nq = t_new * N_HEADS
    return pl.pallas_call(
        functools.partial(_attn_sample_kernel, n_pages=n_pages, ppc=ppc, nseq=nseq, t_new=t_new),
        grid_spec=pltpu.PrefetchScalarGridSpec(
            num_scalar_prefetch=1,
            grid=(nseq,),
            in_specs=[tok_spec, tok_spec, tok_spec, any_spec, any_spec],
            out_specs=tok_spec,
            scratch_shapes=[pltpu.VMEM((2, ATTN_WIDTH, ppc * PAGE_SIZE), F32),
                            pltpu.VMEM((nq, past), F32),
                            pltpu.VMEM((nq, past), BF16),
                            pltpu.VMEM((LANES, ATTN_WIDTH), F32),
                            pltpu.SemaphoreType.DMA((2,))]),
        out_shape=jax.ShapeDtypeStruct((nseq, t_new, ATTN_WIDTH), BF16),
        compiler_params=_params("arbitrary"),
        name="attn_sample",
    )(page_table, q, k_new, v_new, cache_kt, cache_vt)


def _ln_swish(y, g, b):
    mu = jnp.mean(y, axis=-1, keepdims=True)
    yc = y - mu
    yn = yc * lax.rsqrt(jnp.mean(yc * yc, axis=-1, keepdims=True) + NORM_EPS) * g + b
    return yn * jax.nn.sigmoid(yn)


CONV_HALO = 32
CONV_CHUNK = 64


def _conv_prompt_kernel(u_ref, prev_ref, w_ref, bd_ref, g_ref, bl_ref, c_ref, tail_ref, ext_ref,
                        *, tc, last):
    i = pl.program_id(1)
    hist = CONV_K - 1
    ext_ref[0:CONV_HALO, :] = jnp.where(i > 0, prev_ref[0], 0.0)
    ext_ref[CONV_HALO:CONV_HALO + tc, :] = u_ref[0]
    first = CONV_HALO - hist
    for r in range(tc // CONV_CHUNK):
        acc = jnp.zeros((CONV_CHUNK, CONV_WIDTH), F32)
        for k in range(CONV_K):
            win = ext_ref[first + r * CONV_CHUNK + k:first + (r + 1) * CONV_CHUNK + k, :]
            acc = acc + win * w_ref[k:k + 1, :]
        y = _ln_swish(acc + bd_ref[...], g_ref[...], bl_ref[...])
        c_ref[0, r * CONV_CHUNK:(r + 1) * CONV_CHUNK, :] = y.astype(c_ref.dtype)

    @pl.when(i == last)
    def _():
        tail_ref[0] = ext_ref[CONV_HALO + tc - hist:CONV_HALO + tc, :]


def _conv_prompt(u, w_dw, b_dw, g_ln, b_ln, tc):
    b, s, _ = u.shape
    nt = s // tc
    halo_per_tile = tc // CONV_HALO
    vec = pl.BlockSpec((1, CONV_WIDTH), lambda bi, i: (0, 0))
    return pl.pallas_call(
        functools.partial(_conv_prompt_kernel, tc=tc, last=nt - 1),
        grid=(b, nt),
        in_specs=[pl.BlockSpec((1, tc, CONV_WIDTH), lambda bi, i: (bi, i, 0)),
                  pl.BlockSpec((1, CONV_HALO, CONV_WIDTH),
                               lambda bi, i: (bi, jnp.maximum(i * halo_per_tile - 1, 0), 0)),
                  pl.BlockSpec((CONV_K, CONV_WIDTH), lambda bi, i: (0, 0)),
                  vec, vec, vec],
        out_specs=[pl.BlockSpec((1, tc, CONV_WIDTH), lambda bi, i: (bi, i, 0)),
                   pl.BlockSpec((1, CONV_K - 1, CONV_WIDTH), lambda bi, i: (bi, 0, 0))],
        out_shape=[jax.ShapeDtypeStruct((b, s, CONV_WIDTH), BF16),
                   jax.ShapeDtypeStruct((b, CONV_K - 1, CONV_WIDTH), F32)],
        scratch_shapes=[pltpu.VMEM((CONV_HALO + tc, CONV_WIDTH), F32)],
        compiler_params=_params("arbitrary", "arbitrary"),
        name="conv_prompt",
    )(u, u, w_dw, b_dw, g_ln, b_ln)


def _conv_sample_kernel(u_ref, hist_ref, w_ref, bd_ref, g_ref, bl_ref, c_ref, tail_ref, ext_ref,
                        *, t_new):
    hist = CONV_K - 1
    ext_ref[:, 0:hist, :] = hist_ref[...]
    ext_ref[:, hist:hist + t_new, :] = u_ref[...]
    w = w_ref[...]
    for t in range(t_new):
        y = jnp.sum(ext_ref[:, t:t + CONV_K, :] * w[None], axis=1) + bd_ref[...]
        c_ref[:, t, :] = _ln_swish(y, g_ref[...], bl_ref[...]).astype(c_ref.dtype)
    tail_ref[...] = ext_ref[:, t_new:t_new + hist, :]


def _conv_sample(u, hist, w_dw, b_dw, g_ln, b_ln):
    nseq, t_new, _ = u.shape
    ns = min(SAMPLE_CONV_SEQS, nseq)
    vec = pl.BlockSpec((1, CONV_WIDTH), lambda i: (0, 0))
    tok = pl.BlockSpec((ns, t_new, CONV_WIDTH), lambda i: (i, 0, 0))
    his = pl.BlockSpec((ns, CONV_K - 1, CONV_WIDTH), lambda i: (i, 0, 0))
    return pl.pallas_call(
        functools.partial(_conv_sample_kernel, t_new=t_new),
        grid=(nseq // ns,),
        in_specs=[tok, his, pl.BlockSpec((CONV_K, CONV_WIDTH), lambda i: (0, 0)), vec, vec, vec],
        out_specs=[tok, his],
        out_shape=[jax.ShapeDtypeStruct((nseq, t_new, CONV_WIDTH), BF16),
                   jax.ShapeDtypeStruct((nseq, CONV_K - 1, CONV_WIDTH), F32)],
        scratch_shapes=[pltpu.VMEM((ns, CONV_K - 1 + t_new, CONV_WIDTH), F32)],
        compiler_params=_params("arbitrary"),
        name="conv_sample",
    )(u, hist, w_dw, b_dw, g_ln, b_ln)


def _ffn_kernel(x_ref, a_ref, c_ref, woa_ref, woc_ref, g2_ref, wg_ref, wu_ref, wd_ref, gf_ref,
                y_ref, *, final_norm):
    mix = jnp.dot(a_ref[...], woa_ref[...], preferred_element_type=F32)
    mix = mix + jnp.dot(c_ref[...], woc_ref[...], preferred_element_type=F32)
    x1 = x_ref[...] + mix
    h2 = _rms(x1, g2_ref[...]).astype(BF16)
    gate = jnp.dot(h2, wg_ref[...], preferred_element_type=F32)
    up = jnp.dot(h2, wu_ref[...], preferred_element_type=F32)
    ff = (gate * jax.nn.sigmoid(gate) * up).astype(BF16)
    x2 = x1 + jnp.dot(ff, wd_ref[...], preferred_element_type=F32)
    y_ref[...] = _rms(x2, gf_ref[...]) if final_norm else x2


def _ffn(x2d, a2d, c2d, woa, woc, g2, wg, wu, wd, gf, tm, final_norm):
    n = x2d.shape[0]
    d_ff = wg.shape[1]

    def resident(shape):
        return pl.BlockSpec(shape, lambda i: (0, 0), pipeline_mode=pl.Buffered(1))

    return pl.pallas_call(
        functools.partial(_ffn_kernel, final_norm=final_norm),
        grid=(n // tm,),
        in_specs=[pl.BlockSpec((tm, D_MODEL), lambda i: (i, 0)),
                  pl.BlockSpec((tm, ATTN_WIDTH), lambda i: (i, 0)),
                  pl.BlockSpec((tm, CONV_WIDTH), lambda i: (i, 0)),
                  resident((ATTN_WIDTH, D_MODEL)), resident((CONV_WIDTH, D_MODEL)),
                  resident((1, D_MODEL)),
                  resident((D_MODEL, d_ff)), resident((D_MODEL, d_ff)), resident((d_ff, D_MODEL)),
                  resident((1, D_MODEL))],
        out_specs=pl.BlockSpec((tm, D_MODEL), lambda i: (i, 0)),
        out_shape=jax.ShapeDtypeStruct((n, D_MODEL), F32),
        compiler_params=_params("arbitrary"),
        name="out_proj_ffn",
    )(x2d, a2d, c2d, woa, woc, g2, wg, wu, wd, gf)


def kernel(x_prompt, x_sample, cache_k, cache_v, state_conv, page_table, g_mix_norm, w_in, w_dw,
           b_dw, g_conv_ln, b_conv_ln, w_out, g_ffn_norm, w_gate, w_up, w_down, g_final):
    depth = w_in.shape[0]
    bsz, seq, _ = x_prompt.shape
    nseq, t_new, _ = x_sample.shape
    n_phys = cache_k.shape[1]
    past = page_table.shape[1] * PAGE_SIZE
    assert seq % MOBA_BLOCK == 0 and seq % ROW_TILE == 0

    def pages_t(cache):
        return jnp.transpose(cache, (0, 1, 3, 4, 2)).reshape(depth * n_phys, ATTN_WIDTH, PAGE_SIZE)
    cache_kt, cache_vt = pages_t(cache_k), pages_t(cache_v)

    tabs_p = _rope_tables(jnp.arange(seq, dtype=jnp.int32))
    pos_s = past + jnp.arange(t_new, dtype=jnp.int32)
    tabs_s = _rope_tables(jnp.tile(pos_s, nseq))
    n_p, n_s = bsz * seq, nseq * t_new
    tm_s = min(ROW_TILE, n_s)
    tf_s = min(FFN_ROW_TILE, n_s)

    yp = x_prompt.reshape(n_p, D_MODEL)
    ys = x_sample.reshape(n_s, D_MODEL)
    row = lambda a: a.reshape(1, -1)
    outs = [[] for _ in range(6)]
    for l in range(depth):
        last = l == depth - 1
        w_in_b = w_in[l].astype(BF16)
        woa, woc = w_out[l, :ATTN_WIDTH].astype(BF16), w_out[l, ATTN_WIDTH:].astype(BF16)
        wg, wu, wd = w_gate[l].astype(BF16), w_up[l].astype(BF16), w_down[l].astype(BF16)
        conv_w = (w_dw[l], row(b_dw[l]), row(g_conv_ln[l]), row(b_conv_ln[l]))
        ffn_w = (woa, woc, row(g_ffn_norm[l]), wg, wu, wd, row(g_final))

        q, k, v, u = _in_proj(yp, row(g_mix_norm[l]), w_in_b, tabs_p, ROW_TILE)
        shp = (bsz, seq, ATTN_WIDTH)
        a = _attn_prompt(q.reshape(shp), k.reshape(shp), v.reshape(shp))
        c, tail = _conv_prompt(u.reshape(bsz, seq, CONV_WIDTH), *conv_w, ROW_TILE)
        yp = _ffn(yp, a.reshape(n_p, ATTN_WIDTH), c.reshape(n_p, CONV_WIDTH), *ffn_w,
                  FFN_ROW_TILE, last)
        outs[0].append(k.reshape(bsz, seq, N_HEADS, HEAD_DIM))
        outs[1].append(v.reshape(bsz, seq, N_HEADS, HEAD_DIM))
        outs[2].append(tail)

        q, k, v, u = _in_proj(ys, row(g_mix_norm[l]), w_in_b, tabs_s, tm_s)
        shp = (nseq, t_new, ATTN_WIDTH)
        a = _attn_sample(q.reshape(shp), k.reshape(shp), v.reshape(shp),
                         cache_kt, cache_vt, page_table + l * n_phys)
        c, tail = _conv_sample(u.reshape(nseq, t_new, CONV_WIDTH), state_conv[l], *conv_w)
        ys = _ffn(ys, a.reshape(n_s, ATTN_WIDTH), c.reshape(n_s, CONV_WIDTH), *ffn_w, tf_s, last)
        outs[3].append(k.reshape(nseq, t_new, N_HEADS, HEAD_DIM))
        outs[4].append(v.reshape(nseq, t_new, N_HEADS, HEAD_DIM))
        outs[5].append(tail)

    kp, vp, cp, ks, vs, cs = (jnp.stack(o) for o in outs)
    return (yp.reshape(bsz, seq, D_MODEL), ys.reshape(nseq, t_new, D_MODEL), kp, vp, cp, ks, vs, cs)
```

```python
import functools

import jax
import jax.numpy as jnp
from jax import lax
from jax.experimental import pallas as pl
from jax.experimental.pallas import tpu as pltpu

F32 = jnp.float32
BF16 = jnp.bfloat16

D_MODEL = 1024
ATTN_WIDTH = 512
CONV_WIDTH = 512
HEAD_DIM = 64
N_HEADS = ATTN_WIDTH // HEAD_DIM
ROT_DIM = HEAD_DIM // 4
ROPE_THETA = 500000.0
MOBA_BLOCK = 256
MOBA_TOPK = 3
CONV_K = 31
PAGE_SIZE = 128
NORM_EPS = 1e-6
IN_COLS = 3 * ATTN_WIDTH + 2 * CONV_WIDTH

LANES = 128
SUBLANES = 8
HEADS_PER_SLAB = LANES // HEAD_DIM
MASKED = -1e30
VMEM_LIMIT = 56 * 1024 * 1024

ROW_TILE = 512
FFN_ROW_TILE = 256
SAMPLE_PAGES_PER_CHUNK = 16
SAMPLE_SLOTS = 4
SAMPLE_CONV_SEQS = 8


def _params(*sem):
    return pltpu.CompilerParams(dimension_semantics=sem, vmem_limit_bytes=VMEM_LIMIT)


def _rms(x, g):
    return x * lax.rsqrt(jnp.mean(x * x, axis=-1, keepdims=True) + NORM_EPS) * g


def _rope_tables(pos):
    inv = ROPE_THETA ** (-jnp.arange(0, ROT_DIM, 2, dtype=F32) / ROT_DIM)
    ang = pos.astype(F32)[:, None] * inv[None, :]
    cos, sin = jnp.cos(ang), jnp.sin(ang)
    t, half = pos.shape[0], ROT_DIM // 2
    rest = HEAD_DIM - ROT_DIM
    c = jnp.concatenate([cos, cos, jnp.ones((t, rest), F32)], axis=1)
    lo = jnp.concatenate([-sin, jnp.zeros((t, half + rest), F32)], axis=1)
    hi = jnp.concatenate([jnp.zeros((t, half), F32), sin, jnp.zeros((t, rest), F32)], axis=1)
    return tuple(jnp.tile(a, (1, HEADS_PER_SLAB)) for a in (c, lo, hi))


def _in_proj_kernel(x_ref, g_ref, w_ref, cos_ref, lo_ref, hi_ref, q_ref, k_ref, v_ref, u_ref,
                    *, channel_major):
    h = _rms(x_ref[...], g_ref[...])
    proj = jnp.dot(h.astype(BF16), w_ref[...], preferred_element_type=F32)
    cos, lo, hi = cos_ref[...], lo_ref[...], hi_ref[...]
    half = ROT_DIM // 2

    def rope(xs):
        return xs * cos + pltpu.roll(xs, LANES - half, 1) * lo + pltpu.roll(xs, half, 1) * hi

    def put(ref, s, val):
        if channel_major:
            ref[0, s * LANES:(s + 1) * LANES, :] = val.T
        else:
            ref[:, s * LANES:(s + 1) * LANES] = val

    a = ATTN_WIDTH
    for s in range(a // LANES):
        put(q_ref, s, rope(proj[:, s * LANES:(s + 1) * LANES]))
        put(k_ref, s, rope(proj[:, a + s * LANES:a + (s + 1) * LANES]))
        put(v_ref, s, proj[:, 2 * a + s * LANES:2 * a + (s + 1) * LANES])
    ga = proj[:, 3 * a:3 * a + CONV_WIDTH]
    gb = proj[:, 3 * a + CONV_WIDTH:]
    u_ref[...] = ga * jax.nn.sigmoid(gb)


def _in_proj(x2d, g, w_bf16, tables, tm, seqs=None):
    n = x2d.shape[0]
    table_tiles = tables[0].shape[0] // tm
    tab_spec = pl.BlockSpec((tm, LANES), lambda i: (i % table_tiles, 0))
    row_spec = pl.BlockSpec((tm, ATTN_WIDTH), lambda i: (i, 0))
    row_sds = jax.ShapeDtypeStruct((n, ATTN_WIDTH), F32)
    if seqs is None:
        qkv_spec, qkv_sds = row_spec, row_sds
    else:
        tiles_per_seq = n // seqs // tm
        qkv_spec = pl.BlockSpec((1, ATTN_WIDTH, tm),
                                lambda i: (i // tiles_per_seq, 0, i % tiles_per_seq))
        qkv_sds = jax.ShapeDtypeStruct((seqs, ATTN_WIDTH, n // seqs), F32)
    return pl.pallas_call(
        functools.partial(_in_proj_kernel, channel_major=seqs is not None),
        grid=(n // tm,),
        in_specs=[pl.BlockSpec((tm, D_MODEL), lambda i: (i, 0)),
                  pl.BlockSpec((1, D_MODEL), lambda i: (0, 0)),
                  pl.BlockSpec((D_MODEL, IN_COLS), lambda i: (0, 0)),
                  tab_spec, tab_spec, tab_spec],
        out_specs=[qkv_spec] * 3 + [row_spec],
        out_shape=[qkv_sds] * 3 + [row_sds],
        compiler_params=_params("arbitrary"),
        name="in_proj",
    )(x2d, g, w_bf16, *tables)


def _topk_bias(gate, valid, rown, n_rows):
    neg = jnp.finfo(F32).min
    g = jnp.where(valid, gate, neg)
    sel = jnp.zeros(gate.shape, jnp.bool_)
    for _ in range(min(MOBA_TOPK, n_rows)):
        m = jnp.max(g, axis=0, keepdims=True)
        first = jnp.min(jnp.where(g == m, rown, n_rows), axis=0, keepdims=True)
        pick = rown == first
        sel = sel | pick
        g = jnp.where(pick, -jnp.inf, g)
    return jnp.where(sel & valid, 0.0, MASKED)


ATTN_CHAINS = 2
ATTN_CHAIN_BLOCKS = 1
ACC_ROWS = HEAD_DIM + 16
LOG2E = 1.4426950408889634


def _attn_prompt_kernel(q_ref, kt_ref, vt_in_ref, ko_ref, vo_ref, o_ref,
                        ka_ref, vt_ref, km_ref, qa_ref, sa_ref, sb_ref, *, nb):
    i = pl.program_id(2)
    blk = MOBA_BLOCK
    grp = ATTN_CHAIN_BLOCKS * blk
    step_blocks = ATTN_CHAINS * ATTN_CHAIN_BLOCKS
    heads = range(HEADS_PER_SLAB)
    ones = jnp.ones((ACC_ROWS - HEAD_DIM, grp), BF16)

    @pl.when(i == 0)
    def _():
        lane = lax.broadcasted_iota(jnp.int32, (blk, LANES), 1)
        chan = lax.broadcasted_iota(jnp.int32, (1, LANES), 1) // HEAD_DIM

        def stage_k(j, c):
            rows = pl.ds(pl.multiple_of(j * blk, blk), blk)
            kblk = kt_ref[0, :, rows].T
            ka_ref[rows, 0:LANES] = kblk.astype(BF16)
            ka_ref[rows, LANES:2 * LANES] = jnp.where(lane == j, 1.0, 0.0).astype(BF16)
            mean = jnp.mean(kblk, axis=0, keepdims=True)
            for h in heads:
                km_ref[pl.ds(h * nb + j, 1), :] = jnp.where(chan == h, mean, 0.0)
            return c
        lax.fori_loop(0, nb, stage_k, 0)

        def stage_v(g, c):
            cols = pl.ds(pl.multiple_of(g * grp, grp), grp)
            vt = vt_in_ref[0, :, cols].astype(BF16)
            for h in heads:
                vt_ref[g, h, 0:HEAD_DIM, :] = vt[h * HEAD_DIM:(h + 1) * HEAD_DIM, :]
                vt_ref[g, h, HEAD_DIM:ACC_ROWS, :] = ones
            return c
        lax.fori_loop(0, nb // ATTN_CHAIN_BLOCKS, stage_v, 0)

    q_t = q_ref[0]
    gates = jnp.dot(km_ref[...], q_t, precision=lax.Precision.HIGHEST,
                    preferred_element_type=F32)
    k_own = ko_ref[0].T.astype(BF16)
    vt_own = vo_ref[0].astype(BF16)
    qs_t = q_t * (LOG2E * HEAD_DIM ** -0.5)
    drow = lax.broadcasted_iota(jnp.int32, q_t.shape, 0) // HEAD_DIM
    qh_t = [jnp.where(drow == h, qs_t, 0.0).astype(BF16) for h in heads]
    s_own = [jnp.dot(k_own, qh_t[h], preferred_element_type=F32) for h in heads]

    rown = lax.broadcasted_iota(jnp.int32, (nb, blk), 0)
    for h in heads:
        bias = _topk_bias(gates[h * nb:(h + 1) * nb], rown < i, rown, nb)
        qa_ref[h, 0:LANES, :] = qh_t[h]
        qa_ref[h, LANES:LANES + nb, :] = bias.astype(BF16)
        if nb < LANES:
            qa_ref[h, LANES + nb:2 * LANES, :] = jnp.zeros((LANES - nb, blk), BF16)

    def score_step(s_ref, step):
        for c in range(ATTN_CHAINS):
            g = step * ATTN_CHAINS + c
            k_rows = ka_ref[pl.ds(pl.multiple_of(g * grp, grp), grp), :]
            for h in heads:
                s_ref[c, h] = jnp.dot(k_rows, qa_ref[h], preferred_element_type=F32)

    def softmax_step(s_ref, step, state):
        out = []
        for h in heads:
            chains = []
            for c in range(ATTN_CHAINS):
                m, acc = state[h][c]
                s = s_ref[c, h]
                m_new = jnp.maximum(m, jnp.max(s, axis=0, keepdims=True))
                p = jnp.exp2(s - m_new).astype(BF16)
                pv = jnp.dot(vt_ref[step * ATTN_CHAINS + c, h], p, preferred_element_type=F32)
                chains.append((m_new, jnp.exp2(m - m_new) * acc + pv))
            out.append(tuple(chains))
        return tuple(out)

    n_steps_all = nb // step_blocks

    def sweep(u, state):
        score_step(sb_ref, 2 * u + 1)
        state = softmax_step(sa_ref, 2 * u, state)
        score_step(sa_ref, jnp.minimum(2 * u + 2, n_steps_all - 1))
        return softmax_step(sb_ref, 2 * u + 1, state)

    score_step(sa_ref, 0)
    causal = (lax.broadcasted_iota(jnp.int32, (blk, blk), 0)
              <= lax.broadcasted_iota(jnp.int32, (blk, blk), 1))
    init = []
    for h in heads:
        s = jnp.where(causal, s_own[h], MASKED)
        m = jnp.max(s, axis=0, keepdims=True)
        p = jnp.exp2(s - m).astype(BF16)
        vt_ext = jnp.concatenate([vt_own[h * HEAD_DIM:(h + 1) * HEAD_DIM, :], ones[:, :blk]], axis=0)
        acc = jnp.dot(vt_ext, p, preferred_element_type=F32)
        init.append(((m, acc),) + ((m, jnp.zeros_like(acc)),) * (ATTN_CHAINS - 1))
    trip_blocks = 2 * step_blocks
    state = lax.fori_loop(0, (i + trip_blocks - 1) // trip_blocks, sweep, tuple(init))
    outs = []
    for h in heads:
        m_all = functools.reduce(jnp.maximum, [m for m, _ in state[h]])
        acc = sum(jnp.exp2(m - m_all) * a for m, a in state[h])
        outs.append(acc[0:HEAD_DIM] * (1.0 / acc[HEAD_DIM:HEAD_DIM + 1]))
    o_ref[0] = jnp.concatenate(outs, axis=0).T.astype(o_ref.dtype)


def _attn_prompt(q_t, k_t, v_t):
    b, _, s = q_t.shape
    nb = s // MOBA_BLOCK
    assert nb <= LANES and nb % (2 * ATTN_CHAINS * ATTN_CHAIN_BLOCKS) == 0
    slabs = ATTN_WIDTH // LANES
    grp = ATTN_CHAIN_BLOCKS * MOBA_BLOCK
    all_spec = pl.BlockSpec((1, LANES, s), lambda bi, hp, i: (bi, hp, 0))
    own_spec = pl.BlockSpec((1, LANES, MOBA_BLOCK), lambda bi, hp, i: (bi, hp, i))
    score_buf = pltpu.VMEM((ATTN_CHAINS, HEADS_PER_SLAB, grp, MOBA_BLOCK), F32)
    return pl.pallas_call(
        functools.partial(_attn_prompt_kernel, nb=nb),
        grid=(b, slabs, nb),
        in_specs=[own_spec, all_spec, all_spec, own_spec, own_spec],
        out_specs=pl.BlockSpec((1, MOBA_BLOCK, LANES), lambda bi, hp, i: (bi, i, hp)),
        out_shape=jax.ShapeDtypeStruct((b, s, ATTN_WIDTH), BF16),
        scratch_shapes=[pltpu.VMEM((s, 2 * LANES), BF16),
                        pltpu.VMEM((nb // ATTN_CHAIN_BLOCKS, HEADS_PER_SLAB, ACC_ROWS, grp), BF16),
                        pltpu.VMEM((HEADS_PER_SLAB * nb, LANES), F32),
                        pltpu.VMEM((HEADS_PER_SLAB, 2 * LANES, MOBA_BLOCK), BF16),
                        score_buf, score_buf],
        compiler_params=_params("arbitrary", "arbitrary", "arbitrary"),
        name="attn_prompt",
    )(q_t, k_t, v_t, k_t, v_t)


def _attn_sample_kernel(pt_ref, q_ref, kn_ref, vn_ref, ck_ref, cv_ref, o_ref,
                        buf_ref, s_ref, p_ref, own_ref, sem, *, n_pages, ppc, slots, nseq, t_new):
    b = pl.program_id(0)
    blk = MOBA_BLOCK
    nck = n_pages // ppc
    toks = ppc * PAGE_SIZE
    nbk = n_pages * PAGE_SIZE // blk
    nchunks = 2 * nck
    nq = t_new * N_HEADS
    nt = (((1,), (1,)), ((), ()))

    def copies(bb, c, slot):
        src = ck_ref if c < nck else cv_ref
        base = (c % nck) * ppc
        return [pltpu.make_async_copy(src.at[pt_ref[bb, base + p]],
                                      buf_ref.at[slot, :, pl.ds(p * PAGE_SIZE, PAGE_SIZE)],
                                      sem.at[slot]) for p in range(ppc)]

    ahead = slots - 1

    @pl.when(b == 0)
    def _():
        for c in range(ahead):
            for cp in copies(b, c, c % slots):
                cp.start()

    q = q_ref[0]
    col_head = lax.broadcasted_iota(jnp.int32, (N_HEADS, ATTN_WIDTH), 1) // HEAD_DIM
    row_head = lax.broadcasted_iota(jnp.int32, (N_HEADS, ATTN_WIDTH), 0)
    headmask = col_head == row_head
    qexp = jnp.concatenate([jnp.where(headmask, q[t:t + 1, :] * HEAD_DIM ** -0.5, 0.0)
                            for t in range(t_new)], axis=0).astype(BF16)

    acc = m = l = None
    for c in range(nchunks):
        slot = c % slots
        for cp in copies(b, c, slot):
            cp.wait()
        nxt = c + ahead
        if nxt < nchunks:
            for cp in copies(b, nxt, nxt % slots):
                cp.start()
        else:
            @pl.when(b + 1 < nseq)
            def _():
                for cp in copies(b + 1, nxt - nchunks, nxt % slots):
                    cp.start()

        if c < nck:
            s_ref[:, c * toks:(c + 1) * toks] = jnp.dot(qexp, buf_ref[slot].astype(BF16),
                                                        preferred_element_type=F32)
        if c == nck - 1:
            gate = [jnp.sum(s_ref[:, n * blk:(n + 1) * blk], axis=1, keepdims=True)
                    for n in range(nbk)]
            picked = [jnp.zeros((nq, 1), jnp.bool_)] * nbk
            for _ in range(min(MOBA_TOPK, nbk)):
                top = functools.reduce(jnp.maximum, gate)
                found = jnp.zeros((nq, 1), jnp.bool_)
                for n in range(nbk):
                    hit = (gate[n] == top) & jnp.logical_not(found)
                    found = found | hit
                    picked[n] = picked[n] | hit
                    gate[n] = jnp.where(hit, -jnp.inf, gate[n])
            bias = [jnp.where(pk, 0.0, MASKED) for pk in picked]

            own_ref[...] = jnp.zeros(own_ref.shape, F32)
            own_ref[0:t_new, :] = kn_ref[0]
            sn = lax.dot_general(qexp, own_ref[...].astype(BF16), nt,
                                 preferred_element_type=F32)
            key_t = lax.broadcasted_iota(jnp.int32, sn.shape, 1)
            qry_t = lax.broadcasted_iota(jnp.int32, sn.shape, 0) // N_HEADS
            sn = jnp.where((key_t <= qry_t) & (key_t < t_new), sn, MASKED)
            m = jnp.max(sn, axis=1, keepdims=True)
            for n in range(nbk):
                blk_max = jnp.max(s_ref[:, n * blk:(n + 1) * blk], axis=1, keepdims=True)
                m = jnp.maximum(m, blk_max + bias[n])
            pn = jnp.exp(sn - m)
            l = jnp.sum(pn, axis=1, keepdims=True)
            for n in range(nbk):
                p = jnp.exp(s_ref[:, n * blk:(n + 1) * blk] + (bias[n] - m))
                l = l + jnp.sum(p, axis=1, keepdims=True)
                p_ref[:, n * blk:(n + 1) * blk] = p.astype(BF16)
            own_ref[0:t_new, :] = vn_ref[0]
            acc = jnp.dot(pn.astype(BF16), own_ref[...].astype(BF16),
                          preferred_element_type=F32)
        if c >= nck:
            acc = acc + lax.dot_general(p_ref[:, (c - nck) * toks:(c - nck + 1) * toks],
                                        buf_ref[slot].astype(BF16), nt,
                                        preferred_element_type=F32)
    out = acc * (1.0 / l)
    rows = [jnp.sum(jnp.where(headmask, out[t * N_HEADS:(t + 1) * N_HEADS, :], 0.0),
                    axis=0, keepdims=True) for t in range(t_new)]
    o_ref[0] = jnp.concatenate(rows, axis=0).astype(o_ref.dtype)


def _attn_sample(q, k_new, v_new, cache_kt, cache_vt, page_table):
    nseq, t_new, _ = q.shape
    n_pages = page_table.shape[1]
    ppc = min(SAMPLE_PAGES_PER_CHUNK, n_pages)
    past = n_pages * PAGE_SIZE
    assert past % MOBA_BLOCK == 0 and n_pages % ppc == 0
    assert t_new <= LANES
    nchunks = 2 * (n_pages // ppc)
    slots = min(SAMPLE_SLOTS, nchunks)
    assert nchunks % slots == 0
    tok_spec = pl.BlockSpec((1, t_new, ATTN_WIDTH), lambda b, pt: (b, 0, 0))
    any_spec = pl.BlockSpec(memory_space=pl.ANY)
    nq = t_new * N_HEADS
    return pl.pallas_call(
        functools.partial(_attn_sample_kernel, n_pages=n_pages, ppc=ppc, slots=slots, nseq=nseq,
                          t_new=t_new),
        grid_spec=pltpu.PrefetchScalarGridSpec(
            num_scalar_prefetch=1,
            grid=(nseq,),
            in_specs=[tok_spec, tok_spec, tok_spec, any_spec, any_spec],
            out_specs=tok_spec,
            scratch_shapes=[pltpu.VMEM((slots, ATTN_WIDTH, ppc * PAGE_SIZE), F32),
                            pltpu.VMEM((nq, past), F32),
                            pltpu.VMEM((nq, past), BF16),
                            pltpu.VMEM((LANES, ATTN_WIDTH), F32),
                            pltpu.SemaphoreType.DMA((slots,))]),
        out_shape=jax.ShapeDtypeStruct((nseq, t_new, ATTN_WIDTH), BF16),
        compiler_params=_params("arbitrary"),
        name="attn_sample",
    )(page_table, q, k_new, v_new, cache_kt, cache_vt)


def _ln_swish(y, g, b):
    mu = jnp.mean(y, axis=-1, keepdims=True)
    yc = y - mu
    yn = yc * lax.rsqrt(jnp.mean(yc * yc, axis=-1, keepdims=True) + NORM_EPS) * g + b
    return yn * jax.nn.sigmoid(yn)


CONV_HALO = 32
CONV_CHUNK = 64


def _conv_prompt_kernel(u_ref, prev_ref, w_ref, bd_ref, g_ref, bl_ref, c_ref, tail_ref, ext_ref,
                        *, tc, last):
    i = pl.program_id(1)
    hist = CONV_K - 1
    ext_ref[0, 0:CONV_HALO, :] = jnp.where(i > 0, prev_ref[0], 0.0)
    ext_ref[0, CONV_HALO:CONV_HALO + tc, :] = u_ref[0]
    shifted_rows = CONV_HALO + tc - SUBLANES
    for r in range(1, SUBLANES):
        ext_ref[r, 0:shifted_rows, :] = ext_ref[0, r:r + shifted_rows, :]
    first = CONV_HALO - hist
    for c in range(tc // CONV_CHUNK):
        acc = jnp.zeros((CONV_CHUNK, CONV_WIDTH), F32)
        for k in range(CONV_K):
            r = (first + k) % SUBLANES
            start = first + k - r + c * CONV_CHUNK
            acc = acc + ext_ref[r, start:start + CONV_CHUNK, :] * w_ref[k:k + 1, :]
        y = _ln_swish(acc + bd_ref[...], g_ref[...], bl_ref[...])
        c_ref[0, c * CONV_CHUNK:(c + 1) * CONV_CHUNK, :] = y.astype(c_ref.dtype)

    @pl.when(i == last)
    def _():
        tail_ref[0] = ext_ref[0, CONV_HALO + tc - hist:CONV_HALO + tc, :]


def _conv_prompt(u, w_dw, b_dw, g_ln, b_ln, tc):
    b, s, _ = u.shape
    nt = s // tc
    halo_per_tile = tc // CONV_HALO
    vec = pl.BlockSpec((1, CONV_WIDTH), lambda bi, i: (0, 0))
    return pl.pallas_call(
        functools.partial(_conv_prompt_kernel, tc=tc, last=nt - 1),
        grid=(b, nt),
        in_specs=[pl.BlockSpec((1, tc, CONV_WIDTH), lambda bi, i: (bi, i, 0)),
                  pl.BlockSpec((1, CONV_HALO, CONV_WIDTH),
                               lambda bi, i: (bi, jnp.maximum(i * halo_per_tile - 1, 0), 0)),
                  pl.BlockSpec((CONV_K, CONV_WIDTH), lambda bi, i: (0, 0)),
                  vec, vec, vec],
        out_specs=[pl.BlockSpec((1, tc, CONV_WIDTH), lambda bi, i: (bi, i, 0)),
                   pl.BlockSpec((1, CONV_K - 1, CONV_WIDTH), lambda bi, i: (bi, 0, 0))],
        out_shape=[jax.ShapeDtypeStruct((b, s, CONV_WIDTH), BF16),
                   jax.ShapeDtypeStruct((b, CONV_K - 1, CONV_WIDTH), F32)],
        scratch_shapes=[pltpu.VMEM((SUBLANES, CONV_HALO + tc, CONV_WIDTH), F32)],
        compiler_params=_params("arbitrary", "arbitrary"),
        name="conv_prompt",
    )(u, u, w_dw, b_dw, g_ln, b_ln)


def _conv_sample_kernel(u_ref, hist_ref, w_ref, bd_ref, g_ref, bl_ref, c_ref, tail_ref, ext_ref,
                        *, t_new):
    hist = CONV_K - 1
    ext_ref[:, 0:hist, :] = hist_ref[...]
    ext_ref[:, hist:hist + t_new, :] = u_ref[...]
    w = w_ref[...]
    for t in range(t_new):
        y = jnp.sum(ext_ref[:, t:t + CONV_K, :] * w[None], axis=1) + bd_ref[...]
        c_ref[:, t, :] = _ln_swish(y, g_ref[...], bl_ref[...]).astype(c_ref.dtype)
    tail_ref[...] = ext_ref[:, t_new:t_new + hist, :]


def _conv_sample(u, hist, w_dw, b_dw, g_ln, b_ln):
    nseq, t_new, _ = u.shape
    ns = min(SAMPLE_CONV_SEQS, nseq)
    vec = pl.BlockSpec((1, CONV_WIDTH), lambda i: (0, 0))
    tok = pl.BlockSpec((ns, t_new, CONV_WIDTH), lambda i: (i, 0, 0))
    his = pl.BlockSpec((ns, CONV_K - 1, CONV_WIDTH), lambda i: (i, 0, 0))
    return pl.pallas_call(
        functools.partial(_conv_sample_kernel, t_new=t_new),
        grid=(nseq // ns,),
        in_specs=[tok, his, pl.BlockSpec((CONV_K, CONV_WIDTH), lambda i: (0, 0)), vec, vec, vec],
        out_specs=[tok, his],
        out_shape=[jax.ShapeDtypeStruct((nseq, t_new, CONV_WIDTH), BF16),
                   jax.ShapeDtypeStruct((nseq, CONV_K - 1, CONV_WIDTH), F32)],
        scratch_shapes=[pltpu.VMEM((ns, CONV_K - 1 + t_new, CONV_WIDTH), F32)],
        compiler_params=_params("arbitrary"),
        name="conv_sample",
    )(u, hist, w_dw, b_dw, g_ln, b_ln)


def _ffn_kernel(x_ref, a_ref, c_ref, woa_ref, woc_ref, g2_ref, wg_ref, wu_ref, wd_ref, gf_ref,
                y_ref, *, final_norm):
    mix = jnp.dot(a_ref[...], woa_ref[...], preferred_element_type=F32)
    mix = mix + jnp.dot(c_ref[...], woc_ref[...], preferred_element_type=F32)
    x1 = x_ref[...] + mix
    h2 = _rms(x1, g2_ref[...]).astype(BF16)
    gate = jnp.dot(h2, wg_ref[...], preferred_element_type=F32)
    up = jnp.dot(h2, wu_ref[...], preferred_element_type=F32)
    ff = (gate * jax.nn.sigmoid(gate) * up).astype(BF16)
    x2 = x1 + jnp.dot(ff, wd_ref[...], preferred_element_type=F32)
    y_ref[...] = _rms(x2, gf_ref[...]) if final_norm else x2


def _ffn(x2d, a2d, c2d, woa, woc, g2, wg, wu, wd, gf, tm, final_norm):
    n = x2d.shape[0]
    d_ff = wg.shape[1]

    def resident(shape):
        return pl.BlockSpec(shape, lambda i: (0, 0), pipeline_mode=pl.Buffered(1))

    return pl.pallas_call(
        functools.partial(_ffn_kernel, final_norm=final_norm),
        grid=(n // tm,),
        in_specs=[pl.BlockSpec((tm, D_MODEL), lambda i: (i, 0)),
                  pl.BlockSpec((tm, ATTN_WIDTH), lambda i: (i, 0)),
                  pl.BlockSpec((tm, CONV_WIDTH), lambda i: (i, 0)),
                  resident((ATTN_WIDTH, D_MODEL)), resident((CONV_WIDTH, D_MODEL)),
                  resident((1, D_MODEL)),
                  resident((D_MODEL, d_ff)), resident((D_MODEL, d_ff)), resident((d_ff, D_MODEL)),
                  resident((1, D_MODEL))],
        out_specs=pl.BlockSpec((tm, D_MODEL), lambda i: (i, 0)),
        out_shape=jax.ShapeDtypeStruct((n, D_MODEL), F32),
        compiler_params=_params("arbitrary"),
        name="out_proj_ffn",
    )(x2d, a2d, c2d, woa, woc, g2, wg, wu, wd, gf)


def kernel(x_prompt, x_sample, cache_k, cache_v, state_conv, page_table, g_mix_norm, w_in, w_dw,
           b_dw, g_conv_ln, b_conv_ln, w_out, g_ffn_norm, w_gate, w_up, w_down, g_final):
    depth = w_in.shape[0]
    bsz, seq, _ = x_prompt.shape
    nseq, t_new, _ = x_sample.shape
    n_phys = cache_k.shape[1]
    past = page_table.shape[1] * PAGE_SIZE
    assert seq % MOBA_BLOCK == 0 and seq % ROW_TILE == 0

    def pages_t(cache):
        return jnp.transpose(cache, (0, 1, 3, 4, 2)).reshape(depth * n_phys, ATTN_WIDTH, PAGE_SIZE)
    cache_kt, cache_vt = pages_t(cache_k), pages_t(cache_v)

    tabs_p = _rope_tables(jnp.arange(seq, dtype=jnp.int32))
    pos_s = past + jnp.arange(t_new, dtype=jnp.int32)
    tabs_s = _rope_tables(jnp.tile(pos_s, nseq))
    n_p, n_s = bsz * seq, nseq * t_new
    tm_s = min(ROW_TILE, n_s)
    tf_s = min(FFN_ROW_TILE, n_s)

    yp = x_prompt.reshape(n_p, D_MODEL)
    ys = x_sample.reshape(n_s, D_MODEL)
    row = lambda a: a.reshape(1, -1)
    outs = [[] for _ in range(6)]
    for l in range(depth):
        last = l == depth - 1
        w_in_b = w_in[l].astype(BF16)
        woa, woc = w_out[l, :ATTN_WIDTH].astype(BF16), w_out[l, ATTN_WIDTH:].astype(BF16)
        wg, wu, wd = w_gate[l].astype(BF16), w_up[l].astype(BF16), w_down[l].astype(BF16)
        conv_w = (w_dw[l], row(b_dw[l]), row(g_conv_ln[l]), row(b_conv_ln[l]))
        ffn_w = (woa, woc, row(g_ffn_norm[l]), wg, wu, wd, row(g_final))

        q_t, k_t, v_t, u = _in_proj(yp, row(g_mix_norm[l]), w_in_b, tabs_p, ROW_TILE, seqs=bsz)
        a = _attn_prompt(q_t, k_t, v_t)
        c, tail = _conv_prompt(u.reshape(bsz, seq, CONV_WIDTH), *conv_w, ROW_TILE)
        yp = _ffn(yp, a.reshape(n_p, ATTN_WIDTH), c.reshape(n_p, CONV_WIDTH), *ffn_w,
                  FFN_ROW_TILE, last)
        token_major = lambda t: jnp.transpose(t.reshape(bsz, N_HEADS, HEAD_DIM, seq), (0, 3, 1, 2))
        outs[0].append(token_major(k_t))
        outs[1].append(token_major(v_t))
        outs[2].append(tail)

        q, k, v, u = _in_proj(ys, row(g_mix_norm[l]), w_in_b, tabs_s, tm_s)
        shp = (nseq, t_new, ATTN_WIDTH)
        a = _attn_sample(q.reshape(shp), k.reshape(shp), v.reshape(shp),
                         cache_kt, cache_vt, page_table + l * n_phys)
        c, tail = _conv_sample(u.reshape(nseq, t_new, CONV_WIDTH), state_conv[l], *conv_w)
        ys = _ffn(ys, a.reshape(n_s, ATTN_WIDTH), c.reshape(n_s, CONV_WIDTH), *ffn_w, tf_s, last)
        outs[3].append(k.reshape(nseq, t_new, N_HEADS, HEAD_DIM))
        outs[4].append(v.reshape(nseq, t_new, N_HEADS, HEAD_DIM))
        outs[5].append(tail)

    kp, vp, cp, ks, vs, cs = (jnp.stack(o) for o in outs)
    return (yp.reshape(bsz, seq, D_MODEL), ys.reshape(nseq, t_new, D_MODEL), kp, vp, cp, ks, vs, cs)
```

```python
import functools

import jax
import jax.numpy as jnp
from jax import lax
from jax.experimental import pallas as pl
from jax.experimental.pallas import tpu as pltpu

F32 = jnp.float32
BF16 = jnp.bfloat16

D_MODEL = 1024
ATTN_WIDTH = 512
CONV_WIDTH = 512
HEAD_DIM = 64
N_HEADS = ATTN_WIDTH // HEAD_DIM
ROT_DIM = HEAD_DIM // 4
ROPE_THETA = 500000.0
MOBA_BLOCK = 256
MOBA_TOPK = 3
CONV_K = 31
PAGE_SIZE = 128
NORM_EPS = 1e-6
IN_COLS = 3 * ATTN_WIDTH + 2 * CONV_WIDTH

LANES = 128
SUBLANES = 8
HEADS_PER_SLAB = LANES // HEAD_DIM
MASKED = -1e30
VMEM_LIMIT = 56 * 1024 * 1024

ROW_TILE = 512
FFN_ROW_TILE = 512
SAMPLE_PAGES_PER_CHUNK = 16
SAMPLE_SLOTS = 4
SAMPLE_CONV_SEQS = 8


def _params(*sem):
    return pltpu.CompilerParams(dimension_semantics=sem, vmem_limit_bytes=VMEM_LIMIT)


def _rms(x, g):
    return x * lax.rsqrt(jnp.mean(x * x, axis=-1, keepdims=True) + NORM_EPS) * g


def _rope_tables(pos):
    inv = ROPE_THETA ** (-jnp.arange(0, ROT_DIM, 2, dtype=F32) / ROT_DIM)
    ang = pos.astype(F32)[:, None] * inv[None, :]
    cos, sin = jnp.cos(ang), jnp.sin(ang)
    t, half = pos.shape[0], ROT_DIM // 2
    rest = HEAD_DIM - ROT_DIM
    c = jnp.concatenate([cos, cos, jnp.ones((t, rest), F32)], axis=1)
    lo = jnp.concatenate([-sin, jnp.zeros((t, half + rest), F32)], axis=1)
    hi = jnp.concatenate([jnp.zeros((t, half), F32), sin, jnp.zeros((t, rest), F32)], axis=1)
    return tuple(jnp.tile(a, (1, HEADS_PER_SLAB)) for a in (c, lo, hi))


def _in_proj_kernel(x_ref, g_ref, w_ref, cos_ref, lo_ref, hi_ref, q_ref, k_ref, v_ref, u_ref,
                    *, channel_major):
    h = _rms(x_ref[...], g_ref[...])
    proj = jnp.dot(h.astype(BF16), w_ref[...], preferred_element_type=F32)
    cos, lo, hi = cos_ref[...], lo_ref[...], hi_ref[...]
    half = ROT_DIM // 2

    def rope(xs):
        return xs * cos + pltpu.roll(xs, LANES - half, 1) * lo + pltpu.roll(xs, half, 1) * hi

    def put(ref, s, val):
        if channel_major:
            ref[0, s * LANES:(s + 1) * LANES, :] = val.T
        else:
            ref[:, s * LANES:(s + 1) * LANES] = val

    a = ATTN_WIDTH
    for s in range(a // LANES):
        put(q_ref, s, rope(proj[:, s * LANES:(s + 1) * LANES]))
        put(k_ref, s, rope(proj[:, a + s * LANES:a + (s + 1) * LANES]))
        put(v_ref, s, proj[:, 2 * a + s * LANES:2 * a + (s + 1) * LANES])
    ga = proj[:, 3 * a:3 * a + CONV_WIDTH]
    gb = proj[:, 3 * a + CONV_WIDTH:]
    u_ref[...] = ga * jax.nn.sigmoid(gb)


def _in_proj(x2d, g, w_bf16, tables, tm, seqs=None):
    n = x2d.shape[0]
    table_tiles = tables[0].shape[0] // tm
    tab_spec = pl.BlockSpec((tm, LANES), lambda i: (i % table_tiles, 0))
    row_spec = pl.BlockSpec((tm, ATTN_WIDTH), lambda i: (i, 0))
    row_sds = jax.ShapeDtypeStruct((n, ATTN_WIDTH), F32)
    if seqs is None:
        qkv_spec, qkv_sds = row_spec, row_sds
    else:
        tiles_per_seq = n // seqs // tm
        qkv_spec = pl.BlockSpec((1, ATTN_WIDTH, tm),
                                lambda i: (i // tiles_per_seq, 0, i % tiles_per_seq))
        qkv_sds = jax.ShapeDtypeStruct((seqs, ATTN_WIDTH, n // seqs), F32)
    return pl.pallas_call(
        functools.partial(_in_proj_kernel, channel_major=seqs is not None),
        grid=(n // tm,),
        in_specs=[pl.BlockSpec((tm, D_MODEL), lambda i: (i, 0)),
                  pl.BlockSpec((1, D_MODEL), lambda i: (0, 0)),
                  pl.BlockSpec((D_MODEL, IN_COLS), lambda i: (0, 0)),
                  tab_spec, tab_spec, tab_spec],
        out_specs=[qkv_spec] * 3 + [row_spec],
        out_shape=[qkv_sds] * 3 + [row_sds],
        compiler_params=_params("arbitrary"),
        name="in_proj",
    )(x2d, g, w_bf16, *tables)


def _topk_bias(gate, valid, rown, n_rows):
    neg = jnp.finfo(F32).min
    g = jnp.where(valid, gate, neg)
    sel = jnp.zeros(gate.shape, jnp.bool_)
    for _ in range(min(MOBA_TOPK, n_rows)):
        m = jnp.max(g, axis=0, keepdims=True)
        first = jnp.min(jnp.where(g == m, rown, n_rows), axis=0, keepdims=True)
        pick = rown == first
        sel = sel | pick
        g = jnp.where(pick, -jnp.inf, g)
    return jnp.where(sel & valid, 0.0, MASKED)


ATTN_CHAINS = 2
ATTN_CHAIN_BLOCKS = 1
ACC_ROWS = HEAD_DIM + 16
LOG2E = 1.4426950408889634


def _attn_prompt_kernel(q_ref, kt_ref, vt_in_ref, ko_ref, vo_ref, o_ref,
                        ka_ref, vt_ref, km_ref, qa_ref, sa_ref, sb_ref, *, nb):
    i = pl.program_id(2)
    blk = MOBA_BLOCK
    grp = ATTN_CHAIN_BLOCKS * blk
    step_blocks = ATTN_CHAINS * ATTN_CHAIN_BLOCKS
    heads = range(HEADS_PER_SLAB)
    ones = jnp.ones((ACC_ROWS - HEAD_DIM, grp), BF16)

    @pl.when(i == 0)
    def _():
        lane = lax.broadcasted_iota(jnp.int32, (blk, LANES), 1)
        chan = lax.broadcasted_iota(jnp.int32, (1, LANES), 1) // HEAD_DIM

        def stage_k(j, c):
            rows = pl.ds(pl.multiple_of(j * blk, blk), blk)
            kblk = kt_ref[0, :, rows].T
            ka_ref[rows, 0:LANES] = kblk.astype(BF16)
            ka_ref[rows, LANES:2 * LANES] = jnp.where(lane == j, 1.0, 0.0).astype(BF16)
            mean = jnp.mean(kblk, axis=0, keepdims=True)
            for h in heads:
                km_ref[pl.ds(h * nb + j, 1), :] = jnp.where(chan == h, mean, 0.0)
            return c
        lax.fori_loop(0, nb, stage_k, 0)

        def stage_v(g, c):
            cols = pl.ds(pl.multiple_of(g * grp, grp), grp)
            vt = vt_in_ref[0, :, cols].astype(BF16)
            for h in heads:
                vt_ref[g, h, 0:HEAD_DIM, :] = vt[h * HEAD_DIM:(h + 1) * HEAD_DIM, :]
                vt_ref[g, h, HEAD_DIM:ACC_ROWS, :] = ones
            return c
        lax.fori_loop(0, nb // ATTN_CHAIN_BLOCKS, stage_v, 0)

    q_t = q_ref[0]
    gates = jnp.dot(km_ref[...], q_t, precision=lax.Precision.HIGHEST,
                    preferred_element_type=F32)
    k_own = ko_ref[0].T.astype(BF16)
    vt_own = vo_ref[0].astype(BF16)
    qs_t = q_t * (LOG2E * HEAD_DIM ** -0.5)
    drow = lax.broadcasted_iota(jnp.int32, q_t.shape, 0) // HEAD_DIM
    qh_t = [jnp.where(drow == h, qs_t, 0.0).astype(BF16) for h in heads]
    s_own = [jnp.dot(k_own, qh_t[h], preferred_element_type=F32) for h in heads]

    rown = lax.broadcasted_iota(jnp.int32, (nb, blk), 0)
    for h in heads:
        bias = _topk_bias(gates[h * nb:(h + 1) * nb], rown < i, rown, nb)
        qa_ref[h, 0:LANES, :] = qh_t[h]
        qa_ref[h, LANES:LANES + nb, :] = bias.astype(BF16)
        if nb < LANES:
            qa_ref[h, LANES + nb:2 * LANES, :] = jnp.zeros((LANES - nb, blk), BF16)

    def score_step(s_ref, step):
        for c in range(ATTN_CHAINS):
            g = step * ATTN_CHAINS + c
            k_rows = ka_ref[pl.ds(pl.multiple_of(g * grp, grp), grp), :]
            for h in heads:
                s_ref[c, h] = jnp.dot(k_rows, qa_ref[h], preferred_element_type=F32)

    def softmax_step(s_ref, step, state):
        out = []
        for h in heads:
            chains = []
            for c in range(ATTN_CHAINS):
                m, acc = state[h][c]
                s = s_ref[c, h]
                m_new = jnp.maximum(m, jnp.max(s, axis=0, keepdims=True))
                p = jnp.exp2(s - m_new).astype(BF16)
                pv = jnp.dot(vt_ref[step * ATTN_CHAINS + c, h], p, preferred_element_type=F32)
                chains.append((m_new, jnp.exp2(m - m_new) * acc + pv))
            out.append(tuple(chains))
        return tuple(out)

    n_steps_all = nb // step_blocks

    def sweep(u, state):
        score_step(sb_ref, 2 * u + 1)
        state = softmax_step(sa_ref, 2 * u, state)
        score_step(sa_ref, jnp.minimum(2 * u + 2, n_steps_all - 1))
        return softmax_step(sb_ref, 2 * u + 1, state)

    score_step(sa_ref, 0)
    causal = (lax.broadcasted_iota(jnp.int32, (blk, blk), 0)
              <= lax.broadcasted_iota(jnp.int32, (blk, blk), 1))
    init = []
    for h in heads:
        s = jnp.where(causal, s_own[h], MASKED)
        m = jnp.max(s, axis=0, keepdims=True)
        p = jnp.exp2(s - m).astype(BF16)
        vt_ext = jnp.concatenate([vt_own[h * HEAD_DIM:(h + 1) * HEAD_DIM, :], ones[:, :blk]], axis=0)
        acc = jnp.dot(vt_ext, p, preferred_element_type=F32)
        init.append(((m, acc),) + ((m, jnp.zeros_like(acc)),) * (ATTN_CHAINS - 1))
    trip_blocks = 2 * step_blocks
    state = lax.fori_loop(0, (i + trip_blocks - 1) // trip_blocks, sweep, tuple(init))
    outs = []
    for h in heads:
        m_all = functools.reduce(jnp.maximum, [m for m, _ in state[h]])
        acc = sum(jnp.exp2(m - m_all) * a for m, a in state[h])
        outs.append(acc[0:HEAD_DIM] * (1.0 / acc[HEAD_DIM:HEAD_DIM + 1]))
    o_ref[0] = jnp.concatenate(outs, axis=0).T.astype(o_ref.dtype)


def _attn_prompt(q_t, k_t, v_t):
    b, _, s = q_t.shape
    nb = s // MOBA_BLOCK
    assert nb <= LANES and nb % (2 * ATTN_CHAINS * ATTN_CHAIN_BLOCKS) == 0
    slabs = ATTN_WIDTH // LANES
    grp = ATTN_CHAIN_BLOCKS * MOBA_BLOCK
    all_spec = pl.BlockSpec((1, LANES, s), lambda bi, hp, i: (bi, hp, 0))
    own_spec = pl.BlockSpec((1, LANES, MOBA_BLOCK), lambda bi, hp, i: (bi, hp, i))
    score_buf = pltpu.VMEM((ATTN_CHAINS, HEADS_PER_SLAB, grp, MOBA_BLOCK), F32)
    return pl.pallas_call(
        functools.partial(_attn_prompt_kernel, nb=nb),
        grid=(b, slabs, nb),
        in_specs=[own_spec, all_spec, all_spec, own_spec, own_spec],
        out_specs=pl.BlockSpec((1, MOBA_BLOCK, LANES), lambda bi, hp, i: (bi, i, hp)),
        out_shape=jax.ShapeDtypeStruct((b, s, ATTN_WIDTH), BF16),
        scratch_shapes=[pltpu.VMEM((s, 2 * LANES), BF16),
                        pltpu.VMEM((nb // ATTN_CHAIN_BLOCKS, HEADS_PER_SLAB, ACC_ROWS, grp), BF16),
                        pltpu.VMEM((HEADS_PER_SLAB * nb, LANES), F32),
                        pltpu.VMEM((HEADS_PER_SLAB, 2 * LANES, MOBA_BLOCK), BF16),
                        score_buf, score_buf],
        compiler_params=_params("arbitrary", "arbitrary", "arbitrary"),
        name="attn_prompt",
    )(q_t, k_t, v_t, k_t, v_t)


def _attn_sample_kernel(pt_ref, q_ref, kn_ref, vn_ref, ck_ref, cv_ref,
                        u_ref, prev_ref, w_ref, bd_ref, g_ref, bl_ref,
                        o_ref, c_ref, tail_ref,
                        buf_ref, s_ref, p_ref, own_ref, ext_ref, sem,
                        *, n_pages, ppc, slots, nseq, t_new, conv_tiles_per_seq):
    b = pl.program_id(0)
    conv_i = b % conv_tiles_per_seq
    conv_rows = u_ref.shape[1]
    conv_chunks = conv_rows // CONV_CHUNK
    _conv_fill(ext_ref, u_ref[0], prev_ref[0], conv_i == 0)
    blk = MOBA_BLOCK
    nck = n_pages // ppc
    toks = ppc * PAGE_SIZE
    nbk = n_pages * PAGE_SIZE // blk
    nchunks = 2 * nck
    nq = t_new * N_HEADS
    nt = (((1,), (1,)), ((), ()))

    def copies(bb, c, slot):
        src = ck_ref if c < nck else cv_ref
        base = (c % nck) * ppc
        return [pltpu.make_async_copy(src.at[pt_ref[bb, base + p]],
                                      buf_ref.at[slot, :, pl.ds(p * PAGE_SIZE, PAGE_SIZE)],
                                      sem.at[slot]) for p in range(ppc)]

    ahead = slots - 1

    @pl.when(b == 0)
    def _():
        for c in range(ahead):
            for cp in copies(b, c, c % slots):
                cp.start()

    q = q_ref[0]
    col_head = lax.broadcasted_iota(jnp.int32, (N_HEADS, ATTN_WIDTH), 1) // HEAD_DIM
    row_head = lax.broadcasted_iota(jnp.int32, (N_HEADS, ATTN_WIDTH), 0)
    headmask = col_head == row_head
    qexp = jnp.concatenate([jnp.where(headmask, q[t:t + 1, :] * HEAD_DIM ** -0.5, 0.0)
                            for t in range(t_new)], axis=0).astype(BF16)

    acc = m = l = None
    for c in range(nchunks):
        slot = c % slots
        for cp in copies(b, c, slot):
            cp.wait()
        nxt = c + ahead
        if nxt < nchunks:
            for cp in copies(b, nxt, nxt % slots):
                cp.start()
        else:
            @pl.when(b + 1 < nseq)
            def _():
                for cp in copies(b + 1, nxt - nchunks, nxt % slots):
                    cp.start()

        if c < nck:
            s_ref[:, c * toks:(c + 1) * toks] = jnp.dot(qexp, buf_ref[slot].astype(BF16),
                                                        preferred_element_type=F32)
        if c == nck - 1:
            gate = [jnp.sum(s_ref[:, n * blk:(n + 1) * blk], axis=1, keepdims=True)
                    for n in range(nbk)]
            picked = [jnp.zeros((nq, 1), jnp.bool_)] * nbk
            for _ in range(min(MOBA_TOPK, nbk)):
                top = functools.reduce(jnp.maximum, gate)
                found = jnp.zeros((nq, 1), jnp.bool_)
                for n in range(nbk):
                    hit = (gate[n] == top) & jnp.logical_not(found)
                    found = found | hit
                    picked[n] = picked[n] | hit
                    gate[n] = jnp.where(hit, -jnp.inf, gate[n])
            bias = [jnp.where(pk, 0.0, MASKED) for pk in picked]

            own_ref[...] = jnp.zeros(own_ref.shape, F32)
            own_ref[0:t_new, :] = kn_ref[0]
            sn = lax.dot_general(qexp, own_ref[...].astype(BF16), nt,
                                 preferred_element_type=F32)
            key_t = lax.broadcasted_iota(jnp.int32, sn.shape, 1)
            qry_t = lax.broadcasted_iota(jnp.int32, sn.shape, 0) // N_HEADS
            sn = jnp.where((key_t <= qry_t) & (key_t < t_new), sn, MASKED)
            m = jnp.max(sn, axis=1, keepdims=True)
            for n in range(nbk):
                blk_max = jnp.max(s_ref[:, n * blk:(n + 1) * blk], axis=1, keepdims=True)
                m = jnp.maximum(m, blk_max + bias[n])
            pn = jnp.exp(sn - m)
            l = jnp.sum(pn, axis=1, keepdims=True)
            for n in range(nbk):
                p = jnp.exp(s_ref[:, n * blk:(n + 1) * blk] + (bias[n] - m))
                l = l + jnp.sum(p, axis=1, keepdims=True)
                p_ref[:, n * blk:(n + 1) * blk] = p.astype(BF16)
            own_ref[0:t_new, :] = vn_ref[0]
            acc = jnp.dot(pn.astype(BF16), own_ref[...].astype(BF16),
                          preferred_element_type=F32)
        if c >= nck:
            acc = acc + lax.dot_general(p_ref[:, (c - nck) * toks:(c - nck + 1) * toks],
                                        buf_ref[slot].astype(BF16), nt,
                                        preferred_element_type=F32)
        for cc in range(c * conv_chunks // nchunks, (c + 1) * conv_chunks // nchunks):
            _conv_chunk(ext_ref, cc, w_ref, bd_ref, g_ref, bl_ref, c_ref)

    @pl.when(conv_i == conv_tiles_per_seq - 1)
    def _():
        tail_ref[0] = _conv_tail(ext_ref, conv_rows)

    out = acc * (1.0 / l)
    rows = [jnp.sum(jnp.where(headmask, out[t * N_HEADS:(t + 1) * N_HEADS, :], 0.0),
                    axis=0, keepdims=True) for t in range(t_new)]
    o_ref[0] = jnp.concatenate(rows, axis=0).astype(o_ref.dtype)


def _attn_sample_conv_prompt(q, k_new, v_new, cache_kt, cache_vt, page_table,
                             u, w_dw, b_dw, g_ln, b_ln):
    nseq, t_new, _ = q.shape
    n_pages = page_table.shape[1]
    ppc = min(SAMPLE_PAGES_PER_CHUNK, n_pages)
    past = n_pages * PAGE_SIZE
    assert past % MOBA_BLOCK == 0 and n_pages % ppc == 0
    assert t_new <= LANES
    nchunks = 2 * (n_pages // ppc)
    slots = min(SAMPLE_SLOTS, nchunks)
    assert nchunks % slots == 0
    bsz, seq, _ = u.shape
    assert (bsz * seq) % nseq == 0
    tc = bsz * seq // nseq
    assert seq % tc == 0 and tc % CONV_CHUNK == 0 and tc % CONV_HALO == 0
    tps = seq // tc
    halo_per_tile = tc // CONV_HALO

    tok_spec = pl.BlockSpec((1, t_new, ATTN_WIDTH), lambda b, pt: (b, 0, 0))
    any_spec = pl.BlockSpec(memory_space=pl.ANY)
    tile_spec = pl.BlockSpec((1, tc, CONV_WIDTH), lambda b, pt: (b // tps, b % tps, 0))
    prev_spec = pl.BlockSpec(
        (1, CONV_HALO, CONV_WIDTH),
        lambda b, pt: (b // tps, jnp.maximum((b % tps) * halo_per_tile - 1, 0), 0))
    vec = pl.BlockSpec((1, CONV_WIDTH), lambda b, pt: (0, 0))
    nq = t_new * N_HEADS
    return pl.pallas_call(
        functools.partial(_attn_sample_kernel, n_pages=n_pages, ppc=ppc, slots=slots, nseq=nseq,
                          t_new=t_new, conv_tiles_per_seq=tps),
        grid_spec=pltpu.PrefetchScalarGridSpec(
            num_scalar_prefetch=1,
            grid=(nseq,),
            in_specs=[tok_spec, tok_spec, tok_spec, any_spec, any_spec,
                      tile_spec, prev_spec,
                      pl.BlockSpec((CONV_K, CONV_WIDTH), lambda b, pt: (0, 0)), vec, vec, vec],
            out_specs=[tok_spec, tile_spec,
                       pl.BlockSpec((1, CONV_K - 1, CONV_WIDTH), lambda b, pt: (b // tps, 0, 0))],
            scratch_shapes=[pltpu.VMEM((slots, ATTN_WIDTH, ppc * PAGE_SIZE), F32),
                            pltpu.VMEM((nq, past), F32),
                            pltpu.VMEM((nq, past), BF16),
                            pltpu.VMEM((LANES, ATTN_WIDTH), F32),
                            pltpu.VMEM((SUBLANES, CONV_HALO + tc, CONV_WIDTH), F32),
                            pltpu.SemaphoreType.DMA((slots,))]),
        out_shape=[jax.ShapeDtypeStruct((nseq, t_new, ATTN_WIDTH), BF16),
                   jax.ShapeDtypeStruct((bsz, seq, CONV_WIDTH), BF16),
                   jax.ShapeDtypeStruct((bsz, CONV_K - 1, CONV_WIDTH), F32)],
        compiler_params=_params("arbitrary"),
        name="attn_sample_conv_prompt",
    )(page_table, q, k_new, v_new, cache_kt, cache_vt, u, u, w_dw, b_dw, g_ln, b_ln)


def _ln_swish(y, g, b):
    mu = jnp.mean(y, axis=-1, keepdims=True)
    yc = y - mu
    yn = yc * lax.rsqrt(jnp.mean(yc * yc, axis=-1, keepdims=True) + NORM_EPS) * g + b
    return yn * jax.nn.sigmoid(yn)


CONV_HALO = 32
CONV_CHUNK = 64


def _conv_fill(ext_ref, tile, prev, first_tile):
    rows = tile.shape[0]
    ext_ref[0, 0:CONV_HALO, :] = jnp.where(first_tile, 0.0, prev)
    ext_ref[0, CONV_HALO:CONV_HALO + rows, :] = tile
    shifted_rows = CONV_HALO + rows - SUBLANES
    for r in range(1, SUBLANES):
        ext_ref[r, 0:shifted_rows, :] = ext_ref[0, r:r + shifted_rows, :]


def _conv_chunk(ext_ref, c, w_ref, bd_ref, g_ref, bl_ref, c_ref):
    first = CONV_HALO - (CONV_K - 1)
    acc = jnp.zeros((CONV_CHUNK, CONV_WIDTH), F32)
    for k in range(CONV_K):
        r = (first + k) % SUBLANES
        start = first + k - r + c * CONV_CHUNK
        acc = acc + ext_ref[r, start:start + CONV_CHUNK, :] * w_ref[k:k + 1, :]
    y = _ln_swish(acc + bd_ref[...], g_ref[...], bl_ref[...])
    c_ref[0, c * CONV_CHUNK:(c + 1) * CONV_CHUNK, :] = y.astype(c_ref.dtype)


def _conv_tail(ext_ref, rows):
    return ext_ref[0, CONV_HALO + rows - (CONV_K - 1):CONV_HALO + rows, :]


def _conv_sample_kernel(u_ref, hist_ref, w_ref, bd_ref, g_ref, bl_ref, c_ref, tail_ref, ext_ref,
                        *, t_new):
    hist = CONV_K - 1
    ext_ref[:, 0:hist, :] = hist_ref[...]
    ext_ref[:, hist:hist + t_new, :] = u_ref[...]
    w = w_ref[...]
    for t in range(t_new):
        y = jnp.sum(ext_ref[:, t:t + CONV_K, :] * w[None], axis=1) + bd_ref[...]
        c_ref[:, t, :] = _ln_swish(y, g_ref[...], bl_ref[...]).astype(c_ref.dtype)
    tail_ref[...] = ext_ref[:, t_new:t_new + hist, :]


def _conv_sample(u, hist, w_dw, b_dw, g_ln, b_ln):
    nseq, t_new, _ = u.shape
    ns = min(SAMPLE_CONV_SEQS, nseq)
    vec = pl.BlockSpec((1, CONV_WIDTH), lambda i: (0, 0))
    tok = pl.BlockSpec((ns, t_new, CONV_WIDTH), lambda i: (i, 0, 0))
    his = pl.BlockSpec((ns, CONV_K - 1, CONV_WIDTH), lambda i: (i, 0, 0))
    return pl.pallas_call(
        functools.partial(_conv_sample_kernel, t_new=t_new),
        grid=(nseq // ns,),
        in_specs=[tok, his, pl.BlockSpec((CONV_K, CONV_WIDTH), lambda i: (0, 0)), vec, vec, vec],
        out_specs=[tok, his],
        out_shape=[jax.ShapeDtypeStruct((nseq, t_new, CONV_WIDTH), BF16),
                   jax.ShapeDtypeStruct((nseq, CONV_K - 1, CONV_WIDTH), F32)],
        scratch_shapes=[pltpu.VMEM((ns, CONV_K - 1 + t_new, CONV_WIDTH), F32)],
        compiler_params=_params("arbitrary"),
        name="conv_sample",
    )(u, hist, w_dw, b_dw, g_ln, b_ln)


def _ffn_kernel(x_ref, a_ref, c_ref, woa_ref, woc_ref, g2_ref, wg_ref, wu_ref, wd_ref, gf_ref,
                y_ref, *, final_norm):
    mix = jnp.dot(a_ref[...], woa_ref[...], preferred_element_type=F32)
    mix = mix + jnp.dot(c_ref[...], woc_ref[...], preferred_element_type=F32)
    x1 = x_ref[...] + mix
    h2 = _rms(x1, g2_ref[...]).astype(BF16)
    gate = jnp.dot(h2, wg_ref[...], preferred_element_type=F32)
    up = jnp.dot(h2, wu_ref[...], preferred_element_type=F32)
    ff = (gate * jax.nn.sigmoid(gate) * up).astype(BF16)
    x2 = x1 + jnp.dot(ff, wd_ref[...], preferred_element_type=F32)
    y_ref[...] = _rms(x2, gf_ref[...]) if final_norm else x2


def _ffn(x2d, a2d, c2d, woa, woc, g2, wg, wu, wd, gf, tm, final_norm):
    n = x2d.shape[0]
    d_ff = wg.shape[1]

    def resident(shape):
        return pl.BlockSpec(shape, lambda i: (0, 0), pipeline_mode=pl.Buffered(1))

    return pl.pallas_call(
        functools.partial(_ffn_kernel, final_norm=final_norm),
        grid=(n // tm,),
        in_specs=[pl.BlockSpec((tm, D_MODEL), lambda i: (i, 0)),
                  pl.BlockSpec((tm, ATTN_WIDTH), lambda i: (i, 0)),
                  pl.BlockSpec((tm, CONV_WIDTH), lambda i: (i, 0)),
                  resident((ATTN_WIDTH, D_MODEL)), resident((CONV_WIDTH, D_MODEL)),
                  resident((1, D_MODEL)),
                  resident((D_MODEL, d_ff)), resident((D_MODEL, d_ff)), resident((d_ff, D_MODEL)),
                  resident((1, D_MODEL))],
        out_specs=pl.BlockSpec((tm, D_MODEL), lambda i: (i, 0)),
        out_shape=jax.ShapeDtypeStruct((n, D_MODEL), F32),
        compiler_params=_params("arbitrary"),
        name="out_proj_ffn",
    )(x2d, a2d, c2d, woa, woc, g2, wg, wu, wd, gf)


def kernel(x_prompt, x_sample, cache_k, cache_v, state_conv, page_table, g_mix_norm, w_in, w_dw,
           b_dw, g_conv_ln, b_conv_ln, w_out, g_ffn_norm, w_gate, w_up, w_down, g_final):
    depth = w_in.shape[0]
    bsz, seq, _ = x_prompt.shape
    nseq, t_new, _ = x_sample.shape
    n_phys = cache_k.shape[1]
    past = page_table.shape[1] * PAGE_SIZE
    assert seq % MOBA_BLOCK == 0 and seq % ROW_TILE == 0

    def pages_t(cache):
        return jnp.transpose(cache, (0, 1, 3, 4, 2)).reshape(depth * n_phys, ATTN_WIDTH, PAGE_SIZE)
    cache_kt, cache_vt = pages_t(cache_k), pages_t(cache_v)

    tabs_p = _rope_tables(jnp.arange(seq, dtype=jnp.int32))
    pos_s = past + jnp.arange(t_new, dtype=jnp.int32)
    tabs_s = _rope_tables(jnp.tile(pos_s, nseq))
    n_p, n_s = bsz * seq, nseq * t_new
    tm_s = min(ROW_TILE, n_s)
    tf_s = min(FFN_ROW_TILE, n_s)

    yp = x_prompt.reshape(n_p, D_MODEL)
    ys = x_sample.reshape(n_s, D_MODEL)
    row = lambda a: a.reshape(1, -1)
    outs = [[] for _ in range(6)]
    for l in range(depth):
        last = l == depth - 1
        w_in_b = w_in[l].astype(BF16)
        woa, woc = w_out[l, :ATTN_WIDTH].astype(BF16), w_out[l, ATTN_WIDTH:].astype(BF16)
        wg, wu, wd = w_gate[l].astype(BF16), w_up[l].astype(BF16), w_down[l].astype(BF16)
        conv_w = (w_dw[l], row(b_dw[l]), row(g_conv_ln[l]), row(b_conv_ln[l]))
        ffn_w = (woa, woc, row(g_ffn_norm[l]), wg, wu, wd, row(g_final))

        q_t, k_t, v_t, u_p = _in_proj(yp, row(g_mix_norm[l]), w_in_b, tabs_p, ROW_TILE, seqs=bsz)
        q, k, v, u_s = _in_proj(ys, row(g_mix_norm[l]), w_in_b, tabs_s, tm_s)
        a_p = _attn_prompt(q_t, k_t, v_t)
        shp = (nseq, t_new, ATTN_WIDTH)
        a_s, c_p, tail = _attn_sample_conv_prompt(
            q.reshape(shp), k.reshape(shp), v.reshape(shp), cache_kt, cache_vt,
            page_table + l * n_phys, u_p.reshape(bsz, seq, CONV_WIDTH), *conv_w)
        yp = _ffn(yp, a_p.reshape(n_p, ATTN_WIDTH), c_p.reshape(n_p, CONV_WIDTH), *ffn_w,
                  FFN_ROW_TILE, last)
        token_major = lambda t: jnp.transpose(t.reshape(bsz, N_HEADS, HEAD_DIM, seq), (0, 3, 1, 2))
        outs[0].append(token_major(k_t))
        outs[1].append(token_major(v_t))
        outs[2].append(tail)

        c_s, tail = _conv_sample(u_s.reshape(nseq, t_new, CONV_WIDTH), state_conv[l], *conv_w)
        ys = _ffn(ys, a_s.reshape(n_s, ATTN_WIDTH), c_s.reshape(n_s, CONV_WIDTH), *ffn_w, tf_s,
                  last)
        outs[3].append(k.reshape(nseq, t_new, N_HEADS, HEAD_DIM))
        outs[4].append(v.reshape(nseq, t_new, N_HEADS, HEAD_DIM))
        outs[5].append(tail)

    kp, vp, cp, ks, vs, cs = (jnp.stack(o) for o in outs)
    return (yp.reshape(bsz, seq, D_MODEL), ys.reshape(nseq, t_new, D_MODEL), kp, vp, cp, ks, vs, cs)
```

```python
import functools

import jax
import jax.numpy as jnp
from jax import lax
from jax.experimental import pallas as pl
from jax.experimental.pallas import tpu as pltpu

F32 = jnp.float32
BF16 = jnp.bfloat16

D_MODEL = 1024
ATTN_WIDTH = 512
CONV_WIDTH = 512
HEAD_DIM = 64
N_HEADS = ATTN_WIDTH // HEAD_DIM
ROT_DIM = HEAD_DIM // 4
ROPE_THETA = 500000.0
MOBA_BLOCK = 256
MOBA_TOPK = 3
CONV_K = 31
PAGE_SIZE = 128
NORM_EPS = 1e-6
IN_COLS = 3 * ATTN_WIDTH + 2 * CONV_WIDTH

LANES = 128
SUBLANES = 8
HEADS_PER_SLAB = LANES // HEAD_DIM
MASKED = -1e30
VMEM_LIMIT = 56 * 1024 * 1024

ROW_TILE = 512
FFN_ROW_TILE = 512
SAMPLE_PAGES_PER_CHUNK = 16
SAMPLE_SLOTS = 4
SAMPLE_CONV_SEQS = 8


def _params(*sem):
    return pltpu.CompilerParams(dimension_semantics=sem, vmem_limit_bytes=VMEM_LIMIT)


def _rms(x, g):
    return x * lax.rsqrt(jnp.mean(x * x, axis=-1, keepdims=True) + NORM_EPS) * g


def _rope_tables(pos):
    inv = ROPE_THETA ** (-jnp.arange(0, ROT_DIM, 2, dtype=F32) / ROT_DIM)
    ang = pos.astype(F32)[:, None] * inv[None, :]
    cos, sin = jnp.cos(ang), jnp.sin(ang)
    t, half = pos.shape[0], ROT_DIM // 2
    rest = HEAD_DIM - ROT_DIM
    c = jnp.concatenate([cos, cos, jnp.ones((t, rest), F32)], axis=1)
    lo = jnp.concatenate([-sin, jnp.zeros((t, half + rest), F32)], axis=1)
    hi = jnp.concatenate([jnp.zeros((t, half), F32), sin, jnp.zeros((t, rest), F32)], axis=1)
    return tuple(jnp.tile(a, (1, HEADS_PER_SLAB)) for a in (c, lo, hi))


def _in_proj_kernel(x_ref, g_ref, w_ref, cos_ref, lo_ref, hi_ref, q_ref, k_ref, v_ref, u_ref,
                    *, channel_major):
    h = _rms(x_ref[...], g_ref[...])
    proj = jnp.dot(h.astype(BF16), w_ref[...], preferred_element_type=F32)
    cos, lo, hi = cos_ref[...], lo_ref[...], hi_ref[...]
    half = ROT_DIM // 2

    def rope(xs):
        return xs * cos + pltpu.roll(xs, LANES - half, 1) * lo + pltpu.roll(xs, half, 1) * hi

    def put(ref, s, val):
        if channel_major:
            ref[0, s * LANES:(s + 1) * LANES, :] = val.T
        else:
            ref[:, s * LANES:(s + 1) * LANES] = val

    a = ATTN_WIDTH
    for s in range(a // LANES):
        put(q_ref, s, rope(proj[:, s * LANES:(s + 1) * LANES]))
        put(k_ref, s, rope(proj[:, a + s * LANES:a + (s + 1) * LANES]))
        put(v_ref, s, proj[:, 2 * a + s * LANES:2 * a + (s + 1) * LANES])
    ga = proj[:, 3 * a:3 * a + CONV_WIDTH]
    gb = proj[:, 3 * a + CONV_WIDTH:]
    u_ref[...] = ga * jax.nn.sigmoid(gb)


def _in_proj(x2d, g, w_bf16, tables, tm, seqs=None):
    n = x2d.shape[0]
    table_tiles = tables[0].shape[0] // tm
    tab_spec = pl.BlockSpec((tm, LANES), lambda i: (i % table_tiles, 0))
    row_spec = pl.BlockSpec((tm, ATTN_WIDTH), lambda i: (i, 0))
    row_sds = jax.ShapeDtypeStruct((n, ATTN_WIDTH), F32)
    if seqs is None:
        qkv_spec, qkv_sds = row_spec, row_sds
    else:
        tiles_per_seq = n // seqs // tm
        qkv_spec = pl.BlockSpec((1, ATTN_WIDTH, tm),
                                lambda i: (i // tiles_per_seq, 0, i % tiles_per_seq))
        qkv_sds = jax.ShapeDtypeStruct((seqs, ATTN_WIDTH, n // seqs), F32)
    return pl.pallas_call(
        functools.partial(_in_proj_kernel, channel_major=seqs is not None),
        grid=(n // tm,),
        in_specs=[pl.BlockSpec((tm, D_MODEL), lambda i: (i, 0)),
                  pl.BlockSpec((1, D_MODEL), lambda i: (0, 0)),
                  pl.BlockSpec((D_MODEL, IN_COLS), lambda i: (0, 0)),
                  tab_spec, tab_spec, tab_spec],
        out_specs=[qkv_spec] * 3 + [row_spec],
        out_shape=[qkv_sds] * 3 + [row_sds],
        compiler_params=_params("arbitrary"),
        name="in_proj",
    )(x2d, g, w_bf16, *tables)


def _topk_bias(gate, valid, rown, n_rows):
    neg = jnp.finfo(F32).min
    g = jnp.where(valid, gate, neg)
    sel = jnp.zeros(gate.shape, jnp.bool_)
    for _ in range(min(MOBA_TOPK, n_rows)):
        m = jnp.max(g, axis=0, keepdims=True)
        first = jnp.min(jnp.where(g == m, rown, n_rows), axis=0, keepdims=True)
        pick = rown == first
        sel = sel | pick
        g = jnp.where(pick, -jnp.inf, g)
    return jnp.where(sel & valid, 0.0, MASKED)


ATTN_CHAINS = 2
ATTN_CHAIN_BLOCKS = 1
ACC_ROWS = HEAD_DIM + 16
TRIP_STEPS = 4
SHORT_TRIP_STEPS = 2
LOG2E = 1.4426950408889634


def _attn_prompt_kernel(q_ref, kt_ref, vt_in_ref, ko_ref, vo_ref, o_ref,
                        ka_ref, vt_ref, km_ref, qa_ref, sa_ref, sb_ref, *, nb):
    i = pl.program_id(2)
    blk = MOBA_BLOCK
    grp = ATTN_CHAIN_BLOCKS * blk
    step_blocks = ATTN_CHAINS * ATTN_CHAIN_BLOCKS
    heads = range(HEADS_PER_SLAB)
    ones = jnp.ones((ACC_ROWS - HEAD_DIM, grp), BF16)

    @pl.when(i == 0)
    def _():
        lane = lax.broadcasted_iota(jnp.int32, (blk, LANES), 1)
        chan = lax.broadcasted_iota(jnp.int32, (1, LANES), 1) // HEAD_DIM

        def stage_k(j, c):
            rows = pl.ds(pl.multiple_of(j * blk, blk), blk)
            kblk = kt_ref[0, :, rows].T
            ka_ref[rows, 0:LANES] = kblk.astype(BF16)
            ka_ref[rows, LANES:2 * LANES] = jnp.where(lane == j, 1.0, 0.0).astype(BF16)
            mean = jnp.mean(kblk, axis=0, keepdims=True)
            for h in heads:
                km_ref[pl.ds(h * nb + j, 1), :] = jnp.where(chan == h, mean, 0.0)
            return c
        lax.fori_loop(0, nb, stage_k, 0)

        def stage_v(g, c):
            cols = pl.ds(pl.multiple_of(g * grp, grp), grp)
            vt = vt_in_ref[0, :, cols].astype(BF16)
            for h in heads:
                vt_ref[g, h, 0:HEAD_DIM, :] = vt[h * HEAD_DIM:(h + 1) * HEAD_DIM, :]
                vt_ref[g, h, HEAD_DIM:ACC_ROWS, :] = ones
            return c
        lax.fori_loop(0, nb // ATTN_CHAIN_BLOCKS, stage_v, 0)

    q_t = q_ref[0]
    gates = jnp.dot(km_ref[...], q_t, precision=lax.Precision.HIGHEST,
                    preferred_element_type=F32)
    k_own = ko_ref[0].T.astype(BF16)
    vt_own = vo_ref[0].astype(BF16)
    qs_t = q_t * (LOG2E * HEAD_DIM ** -0.5)
    drow = lax.broadcasted_iota(jnp.int32, q_t.shape, 0) // HEAD_DIM
    qh_t = [jnp.where(drow == h, qs_t, 0.0).astype(BF16) for h in heads]
    s_own = [jnp.dot(k_own, qh_t[h], preferred_element_type=F32) for h in heads]

    rown = lax.broadcasted_iota(jnp.int32, (nb, blk), 0)
    for h in heads:
        bias = _topk_bias(gates[h * nb:(h + 1) * nb], rown < i, rown, nb)
        qa_ref[h, 0:LANES, :] = qh_t[h]
        qa_ref[h, LANES:LANES + nb, :] = bias.astype(BF16)
        if nb < LANES:
            qa_ref[h, LANES + nb:2 * LANES, :] = jnp.zeros((LANES - nb, blk), BF16)

    def score_step(s_ref, step):
        rows = ATTN_CHAINS * grp
        k_rows = ka_ref[pl.ds(pl.multiple_of(step * rows, rows), rows), :]
        for h in heads:
            s = jnp.dot(k_rows, qa_ref[h], preferred_element_type=F32)
            for c in range(ATTN_CHAINS):
                s_ref[c, h] = s[c * grp:(c + 1) * grp]

    def softmax_step(s_ref, step, state):
        out = []
        for h in heads:
            chains = []
            for c in range(ATTN_CHAINS):
                m, acc = state[h][c]
                s = s_ref[c, h]
                m_new = jnp.maximum(m, jnp.max(s, axis=0, keepdims=True))
                p = jnp.exp2(s - m_new).astype(BF16)
                pv = jnp.dot(vt_ref[step * ATTN_CHAINS + c, h], p, preferred_element_type=F32)
                chains.append((m_new, jnp.exp2(m - m_new) * acc + pv))
            out.append(tuple(chains))
        return tuple(out)

    n_steps_all = nb // step_blocks

    def trip(first, n_steps, look_ahead, state):
        for t in range(n_steps):
            ahead, cur = (sb_ref, sa_ref) if t % 2 == 0 else (sa_ref, sb_ref)
            if t + 1 < n_steps or look_ahead:
                score_step(ahead, jnp.minimum(first + t + 1, n_steps_all - 1))
            state = softmax_step(cur, first + t, state)
        return state

    score_step(sa_ref, 0)
    causal = (lax.broadcasted_iota(jnp.int32, (blk, blk), 0)
              <= lax.broadcasted_iota(jnp.int32, (blk, blk), 1))
    init = []
    for h in heads:
        s = jnp.where(causal, s_own[h], MASKED)
        m = jnp.max(s, axis=0, keepdims=True)
        p = jnp.exp2(s - m).astype(BF16)
        vt_ext = jnp.concatenate([vt_own[h * HEAD_DIM:(h + 1) * HEAD_DIM, :], ones[:, :blk]], axis=0)
        acc = jnp.dot(vt_ext, p, preferred_element_type=F32)
        init.append(((m, acc),) + ((m, jnp.zeros_like(acc)),) * (ATTN_CHAINS - 1))
    long_blocks, short_blocks = TRIP_STEPS * step_blocks, SHORT_TRIP_STEPS * step_blocks
    rem = i % long_blocks
    n_long = i // long_blocks + (rem > short_blocks).astype(jnp.int32)
    n_short = ((rem > 0) & (rem <= short_blocks)).astype(jnp.int32)
    state = lax.fori_loop(0, n_long, lambda u, st: trip(TRIP_STEPS * u, TRIP_STEPS, True, st),
                          tuple(init))
    state = lax.fori_loop(0, n_short,
                          lambda u, st: trip(TRIP_STEPS * n_long, SHORT_TRIP_STEPS, False, st), state)
    outs = []
    for h in heads:
        m_all = functools.reduce(jnp.maximum, [m for m, _ in state[h]])
        acc = sum(jnp.exp2(m - m_all) * a for m, a in state[h])
        outs.append(acc[0:HEAD_DIM] * (1.0 / acc[HEAD_DIM:HEAD_DIM + 1]))
    o_ref[0] = jnp.concatenate(outs, axis=0).T.astype(o_ref.dtype)


def _attn_prompt(q_t, k_t, v_t):
    b, _, s = q_t.shape
    nb = s // MOBA_BLOCK
    assert nb <= LANES and TRIP_STEPS % 2 == 0
    assert nb % (TRIP_STEPS * ATTN_CHAINS * ATTN_CHAIN_BLOCKS) == 0
    slabs = ATTN_WIDTH // LANES
    grp = ATTN_CHAIN_BLOCKS * MOBA_BLOCK
    all_spec = pl.BlockSpec((1, LANES, s), lambda bi, hp, i: (bi, hp, 0))
    own_spec = pl.BlockSpec((1, LANES, MOBA_BLOCK), lambda bi, hp, i: (bi, hp, i))
    score_buf = pltpu.VMEM((ATTN_CHAINS, HEADS_PER_SLAB, grp, MOBA_BLOCK), F32)
    return pl.pallas_call(
        functools.partial(_attn_prompt_kernel, nb=nb),
        grid=(b, slabs, nb),
        in_specs=[own_spec, all_spec, all_spec, own_spec, own_spec],
        out_specs=pl.BlockSpec((1, MOBA_BLOCK, LANES), lambda bi, hp, i: (bi, i, hp)),
        out_shape=jax.ShapeDtypeStruct((b, s, ATTN_WIDTH), BF16),
        scratch_shapes=[pltpu.VMEM((s, 2 * LANES), BF16),
                        pltpu.VMEM((nb // ATTN_CHAIN_BLOCKS, HEADS_PER_SLAB, ACC_ROWS, grp), BF16),
                        pltpu.VMEM((HEADS_PER_SLAB * nb, LANES), F32),
                        pltpu.VMEM((HEADS_PER_SLAB, 2 * LANES, MOBA_BLOCK), BF16),
                        score_buf, score_buf],
        compiler_params=_params("arbitrary", "arbitrary", "arbitrary"),
        name="attn_prompt",
    )(q_t, k_t, v_t, k_t, v_t)


def _attn_sample_kernel(pt_ref, q_ref, kn_ref, vn_ref, ck_ref, cv_ref,
                        u_ref, prev_ref, w_ref, bd_ref, g_ref, bl_ref,
                        o_ref, c_ref, tail_ref,
                        buf_ref, s_ref, p_ref, own_ref, ext_ref, sem,
                        *, n_pages, ppc, slots, nseq, t_new, conv_tiles_per_seq):
    b = pl.program_id(0)
    conv_i = b % conv_tiles_per_seq
    conv_rows = u_ref.shape[1]
    conv_chunks = conv_rows // CONV_CHUNK
    _conv_fill(ext_ref, u_ref[0], prev_ref[0], conv_i == 0)
    blk = MOBA_BLOCK
    nck = n_pages // ppc
    toks = ppc * PAGE_SIZE
    nbk = n_pages * PAGE_SIZE // blk
    nchunks = 2 * nck
    nq = t_new * N_HEADS
    nt = (((1,), (1,)), ((), ()))

    def copies(bb, c, slot):
        src = ck_ref if c < nck else cv_ref
        base = (c % nck) * ppc
        return [pltpu.make_async_copy(src.at[pt_ref[bb, base + p]],
                                      buf_ref.at[slot, :, pl.ds(p * PAGE_SIZE, PAGE_SIZE)],
                                      sem.at[slot]) for p in range(ppc)]

    ahead = slots - 1

    @pl.when(b == 0)
    def _():
        for c in range(ahead):
            for cp in copies(b, c, c % slots):
                cp.start()

    q = q_ref[0]
    col_head = lax.broadcasted_iota(jnp.int32, (N_HEADS, ATTN_WIDTH), 1) // HEAD_DIM
    row_head = lax.broadcasted_iota(jnp.int32, (N_HEADS, ATTN_WIDTH), 0)
    headmask = col_head == row_head
    qexp = jnp.concatenate([jnp.where(headmask, q[t:t + 1, :] * HEAD_DIM ** -0.5, 0.0)
                            for t in range(t_new)], axis=0).astype(BF16)

    acc = m = l = None
    for c in range(nchunks):
        slot = c % slots
        for cp in copies(b, c, slot):
            cp.wait()
        nxt = c + ahead
        if nxt < nchunks:
            for cp in copies(b, nxt, nxt % slots):
                cp.start()
        else:
            @pl.when(b + 1 < nseq)
            def _():
                for cp in copies(b + 1, nxt - nchunks, nxt % slots):
                    cp.start()

        if c < nck:
            s_ref[:, c * toks:(c + 1) * toks] = jnp.dot(qexp, buf_ref[slot].astype(BF16),
                                                        preferred_element_type=F32)
        if c == nck - 1:
            gate = [jnp.sum(s_ref[:, n * blk:(n + 1) * blk], axis=1, keepdims=True)
                    for n in range(nbk)]
            picked = [jnp.zeros((nq, 1), jnp.bool_)] * nbk
            for _ in range(min(MOBA_TOPK, nbk)):
                top = functools.reduce(jnp.maximum, gate)
                found = jnp.zeros((nq, 1), jnp.bool_)
                for n in range(nbk):
                    hit = (gate[n] == top) & jnp.logical_not(found)
                    found = found | hit
                    picked[n] = picked[n] | hit
                    gate[n] = jnp.where(hit, -jnp.inf, gate[n])
            bias = [jnp.where(pk, 0.0, MASKED) for pk in picked]

            own_ref[...] = jnp.zeros(own_ref.shape, F32)
            own_ref[0:t_new, :] = kn_ref[0]
            sn = lax.dot_general(qexp, own_ref[...].astype(BF16), nt,
                                 preferred_element_type=F32)
            key_t = lax.broadcasted_iota(jnp.int32, sn.shape, 1)
            qry_t = lax.broadcasted_iota(jnp.int32, sn.shape, 0) // N_HEADS
            sn = jnp.where((key_t <= qry_t) & (key_t < t_new), sn, MASKED)
            m = jnp.max(sn, axis=1, keepdims=True)
            for n in range(nbk):
                blk_max = jnp.max(s_ref[:, n * blk:(n + 1) * blk], axis=1, keepdims=True)
                m = jnp.maximum(m, blk_max + bias[n])
            pn = jnp.exp(sn - m)
            l = jnp.sum(pn, axis=1, keepdims=True)
            for n in range(nbk):
                p = jnp.exp(s_ref[:, n * blk:(n + 1) * blk] + (bias[n] - m))
                l = l + jnp.sum(p, axis=1, keepdims=True)
                p_ref[:, n * blk:(n + 1) * blk] = p.astype(BF16)
            own_ref[0:t_new, :] = vn_ref[0]
            acc = jnp.dot(pn.astype(BF16), own_ref[...].astype(BF16),
                          preferred_element_type=F32)
        if c >= nck:
            acc = acc + lax.dot_general(p_ref[:, (c - nck) * toks:(c - nck + 1) * toks],
                                        buf_ref[slot].astype(BF16), nt,
                                        preferred_element_type=F32)
        for cc in range(c * conv_chunks // nchunks, (c + 1) * conv_chunks // nchunks):
            _conv_chunk(ext_ref, cc, w_ref, bd_ref, g_ref, bl_ref, c_ref)

    @pl.when(conv_i == conv_tiles_per_seq - 1)
    def _():
        tail_ref[0] = _conv_tail(ext_ref, conv_rows)

    out = acc * (1.0 / l)
    rows = [jnp.sum(jnp.where(headmask, out[t * N_HEADS:(t + 1) * N_HEADS, :], 0.0),
                    axis=0, keepdims=True) for t in range(t_new)]
    o_ref[0] = jnp.concatenate(rows, axis=0).astype(o_ref.dtype)


def _attn_sample_conv_prompt(q, k_new, v_new, cache_kt, cache_vt, page_table,
                             u, w_dw, b_dw, g_ln, b_ln):
    nseq, t_new, _ = q.shape
    n_pages = page_table.shape[1]
    ppc = min(SAMPLE_PAGES_PER_CHUNK, n_pages)
    past = n_pages * PAGE_SIZE
    assert past % MOBA_BLOCK == 0 and n_pages % ppc == 0
    assert t_new <= LANES
    nchunks = 2 * (n_pages // ppc)
    slots = min(SAMPLE_SLOTS, nchunks)
    assert nchunks % slots == 0
    bsz, seq, _ = u.shape
    assert (bsz * seq) % nseq == 0
    tc = bsz * seq // nseq
    assert seq % tc == 0 and tc % CONV_CHUNK == 0 and tc % CONV_HALO == 0
    tps = seq // tc
    halo_per_tile = tc // CONV_HALO

    tok_spec = pl.BlockSpec((1, t_new, ATTN_WIDTH), lambda b, pt: (b, 0, 0))
    any_spec = pl.BlockSpec(memory_space=pl.ANY)
    tile_spec = pl.BlockSpec((1, tc, CONV_WIDTH), lambda b, pt: (b // tps, b % tps, 0))
    prev_spec = pl.BlockSpec(
        (1, CONV_HALO, CONV_WIDTH),
        lambda b, pt: (b // tps, jnp.maximum((b % tps) * halo_per_tile - 1, 0), 0))
    vec = pl.BlockSpec((1, CONV_WIDTH), lambda b, pt: (0, 0))
    nq = t_new * N_HEADS
    return pl.pallas_call(
        functools.partial(_attn_sample_kernel, n_pages=n_pages, ppc=ppc, slots=slots, nseq=nseq,
                          t_new=t_new, conv_tiles_per_seq=tps),
        grid_spec=pltpu.PrefetchScalarGridSpec(
            num_scalar_prefetch=1,
            grid=(nseq,),
            in_specs=[tok_spec, tok_spec, tok_spec, any_spec, any_spec,
                      tile_spec, prev_spec,
                      pl.BlockSpec((CONV_K, CONV_WIDTH), lambda b, pt: (0, 0)), vec, vec, vec],
            out_specs=[tok_spec, tile_spec,
                       pl.BlockSpec((1, CONV_K - 1, CONV_WIDTH), lambda b, pt: (b // tps, 0, 0))],
            scratch_shapes=[pltpu.VMEM((slots, ATTN_WIDTH, ppc * PAGE_SIZE), F32),
                            pltpu.VMEM((nq, past), F32),
                            pltpu.VMEM((nq, past), BF16),
                            pltpu.VMEM((LANES, ATTN_WIDTH), F32),
                            pltpu.VMEM((SUBLANES, CONV_HALO + tc, CONV_WIDTH), F32),
                            pltpu.SemaphoreType.DMA((slots,))]),
        out_shape=[jax.ShapeDtypeStruct((nseq, t_new, ATTN_WIDTH), BF16),
                   jax.ShapeDtypeStruct((bsz, seq, CONV_WIDTH), BF16),
                   jax.ShapeDtypeStruct((bsz, CONV_K - 1, CONV_WIDTH), F32)],
        compiler_params=_params("arbitrary"),
        name="attn_sample_conv_prompt",
    )(page_table, q, k_new, v_new, cache_kt, cache_vt, u, u, w_dw, b_dw, g_ln, b_ln)


def _ln_swish(y, g, b):
    mu = jnp.mean(y, axis=-1, keepdims=True)
    yc = y - mu
    yn = yc * lax.rsqrt(jnp.mean(yc * yc, axis=-1, keepdims=True) + NORM_EPS) * g + b
    return yn * jax.nn.sigmoid(yn)


CONV_HALO = 32
CONV_CHUNK = 64


def _conv_fill(ext_ref, tile, prev, first_tile):
    rows = tile.shape[0]
    ext_ref[0, 0:CONV_HALO, :] = jnp.where(first_tile, 0.0, prev)
    ext_ref[0, CONV_HALO:CONV_HALO + rows, :] = tile
    shifted_rows = CONV_HALO + rows - SUBLANES
    for r in range(1, SUBLANES):
        ext_ref[r, 0:shifted_rows, :] = ext_ref[0, r:r + shifted_rows, :]


def _conv_chunk(ext_ref, c, w_ref, bd_ref, g_ref, bl_ref, c_ref):
    first = CONV_HALO - (CONV_K - 1)
    acc = jnp.zeros((CONV_CHUNK, CONV_WIDTH), F32)
    for k in range(CONV_K):
        r = (first + k) % SUBLANES
        start = first + k - r + c * CONV_CHUNK
        acc = acc + ext_ref[r, start:start + CONV_CHUNK, :] * w_ref[k:k + 1, :]
    y = _ln_swish(acc + bd_ref[...], g_ref[...], bl_ref[...])
    c_ref[0, c * CONV_CHUNK:(c + 1) * CONV_CHUNK, :] = y.astype(c_ref.dtype)


def _conv_tail(ext_ref, rows):
    return ext_ref[0, CONV_HALO + rows - (CONV_K - 1):CONV_HALO + rows, :]


def _conv_sample_kernel(u_ref, hist_ref, w_ref, bd_ref, g_ref, bl_ref, c_ref, tail_ref, ext_ref,
                        *, t_new):
    hist = CONV_K - 1
    ext_ref[:, 0:hist, :] = hist_ref[...]
    ext_ref[:, hist:hist + t_new, :] = u_ref[...]
    w = w_ref[...]
    for t in range(t_new):
        y = jnp.sum(ext_ref[:, t:t + CONV_K, :] * w[None], axis=1) + bd_ref[...]
        c_ref[:, t, :] = _ln_swish(y, g_ref[...], bl_ref[...]).astype(c_ref.dtype)
    tail_ref[...] = ext_ref[:, t_new:t_new + hist, :]


def _conv_sample(u, hist, w_dw, b_dw, g_ln, b_ln):
    nseq, t_new, _ = u.shape
    ns = min(SAMPLE_CONV_SEQS, nseq)
    vec = pl.BlockSpec((1, CONV_WIDTH), lambda i: (0, 0))
    tok = pl.BlockSpec((ns, t_new, CONV_WIDTH), lambda i: (i, 0, 0))
    his = pl.BlockSpec((ns, CONV_K - 1, CONV_WIDTH), lambda i: (i, 0, 0))
    return pl.pallas_call(
        functools.partial(_conv_sample_kernel, t_new=t_new),
        grid=(nseq // ns,),
        in_specs=[tok, his, pl.BlockSpec((CONV_K, CONV_WIDTH), lambda i: (0, 0)), vec, vec, vec],
        out_specs=[tok, his],
        out_shape=[jax.ShapeDtypeStruct((nseq, t_new, CONV_WIDTH), BF16),
                   jax.ShapeDtypeStruct((nseq, CONV_K - 1, CONV_WIDTH), F32)],
        scratch_shapes=[pltpu.VMEM((ns, CONV_K - 1 + t_new, CONV_WIDTH), F32)],
        compiler_params=_params("arbitrary"),
        name="conv_sample",
    )(u, hist, w_dw, b_dw, g_ln, b_ln)


def _ffn_kernel(x_ref, a_ref, c_ref, woa_ref, woc_ref, g2_ref, wg_ref, wu_ref, wd_ref, gf_ref,
                y_ref, *, final_norm):
    mix = jnp.dot(a_ref[...], woa_ref[...], preferred_element_type=F32)
    mix = mix + jnp.dot(c_ref[...], woc_ref[...], preferred_element_type=F32)
    x1 = x_ref[...] + mix
    h2 = _rms(x1, g2_ref[...]).astype(BF16)
    gate = jnp.dot(h2, wg_ref[...], preferred_element_type=F32)
    up = jnp.dot(h2, wu_ref[...], preferred_element_type=F32)
    ff = (gate * jax.nn.sigmoid(gate) * up).astype(BF16)
    x2 = x1 + jnp.dot(ff, wd_ref[...], preferred_element_type=F32)
    y_ref[...] = _rms(x2, gf_ref[...]) if final_norm else x2


def _ffn(x2d, a2d, c2d, woa, woc, g2, wg, wu, wd, gf, tm, final_norm):
    n = x2d.shape[0]
    d_ff = wg.shape[1]

    def resident(shape):
        return pl.BlockSpec(shape, lambda i: (0, 0), pipeline_mode=pl.Buffered(1))

    return pl.pallas_call(
        functools.partial(_ffn_kernel, final_norm=final_norm),
        grid=(n // tm,),
        in_specs=[pl.BlockSpec((tm, D_MODEL), lambda i: (i, 0)),
                  pl.BlockSpec((tm, ATTN_WIDTH), lambda i: (i, 0)),
                  pl.BlockSpec((tm, CONV_WIDTH), lambda i: (i, 0)),
                  resident((ATTN_WIDTH, D_MODEL)), resident((CONV_WIDTH, D_MODEL)),
                  resident((1, D_MODEL)),
                  resident((D_MODEL, d_ff)), resident((D_MODEL, d_ff)), resident((d_ff, D_MODEL)),
                  resident((1, D_MODEL))],
        out_specs=pl.BlockSpec((tm, D_MODEL), lambda i: (i, 0)),
        out_shape=jax.ShapeDtypeStruct((n, D_MODEL), F32),
        compiler_params=_params("arbitrary"),
        name="out_proj_ffn",
    )(x2d, a2d, c2d, woa, woc, g2, wg, wu, wd, gf)


def kernel(x_prompt, x_sample, cache_k, cache_v, state_conv, page_table, g_mix_norm, w_in, w_dw,
           b_dw, g_conv_ln, b_conv_ln, w_out, g_ffn_norm, w_gate, w_up, w_down, g_final):
    depth = w_in.shape[0]
    bsz, seq, _ = x_prompt.shape
    nseq, t_new, _ = x_sample.shape
    n_phys = cache_k.shape[1]
    past = page_table.shape[1] * PAGE_SIZE
    assert seq % MOBA_BLOCK == 0 and seq % ROW_TILE == 0

    def pages_t(cache):
        return jnp.transpose(cache, (0, 1, 3, 4, 2)).reshape(depth * n_phys, ATTN_WIDTH, PAGE_SIZE)
    cache_kt, cache_vt = pages_t(cache_k), pages_t(cache_v)

    tabs_p = _rope_tables(jnp.arange(seq, dtype=jnp.int32))
    pos_s = past + jnp.arange(t_new, dtype=jnp.int32)
    tabs_s = _rope_tables(jnp.tile(pos_s, nseq))
    n_p, n_s = bsz * seq, nseq * t_new
    tm_s = min(ROW_TILE, n_s)
    tf_s = min(FFN_ROW_TILE, n_s)

    yp = x_prompt.reshape(n_p, D_MODEL)
    ys = x_sample.reshape(n_s, D_MODEL)
    row = lambda a: a.reshape(1, -1)
    outs = [[] for _ in range(6)]
    for l in range(depth):
        last = l == depth - 1
        w_in_b = w_in[l].astype(BF16)
        woa, woc = w_out[l, :ATTN_WIDTH].astype(BF16), w_out[l, ATTN_WIDTH:].astype(BF16)
        wg, wu, wd = w_gate[l].astype(BF16), w_up[l].astype(BF16), w_down[l].astype(BF16)
        conv_w = (w_dw[l], row(b_dw[l]), row(g_conv_ln[l]), row(b_conv_ln[l]))
        ffn_w = (woa, woc, row(g_ffn_norm[l]), wg, wu, wd, row(g_final))

        q_t, k_t, v_t, u_p = _in_proj(yp, row(g_mix_norm[l]), w_in_b, tabs_p, ROW_TILE, seqs=bsz)
        q, k, v, u_s = _in_proj(ys, row(g_mix_norm[l]), w_in_b, tabs_s, tm_s)
        a_p = _attn_prompt(q_t, k_t, v_t)
        shp = (nseq, t_new, ATTN_WIDTH)
        a_s, c_p, tail = _attn_sample_conv_prompt(
            q.reshape(shp), k.reshape(shp), v.reshape(shp), cache_kt, cache_vt,
            page_table + l * n_phys, u_p.reshape(bsz, seq, CONV_WIDTH), *conv_w)
        yp = _ffn(yp, a_p.reshape(n_p, ATTN_WIDTH), c_p.reshape(n_p, CONV_WIDTH), *ffn_w,
                  FFN_ROW_TILE, last)
        token_major = lambda t: jnp.transpose(t.reshape(bsz, N_HEADS, HEAD_DIM, seq), (0, 3, 1, 2))
        outs[0].append(token_major(k_t))
        outs[1].append(token_major(v_t))
        outs[2].append(tail)

        c_s, tail = _conv_sample(u_s.reshape(nseq, t_new, CONV_WIDTH), state_conv[l], *conv_w)
        ys = _ffn(ys, a_s.reshape(n_s, ATTN_WIDTH), c_s.reshape(n_s, CONV_WIDTH), *ffn_w, tf_s,
                  last)
        outs[3].append(k.reshape(nseq, t_new, N_HEADS, HEAD_DIM))
        outs[4].append(v.reshape(nseq, t_new, N_HEADS, HEAD_DIM))
        outs[5].append(tail)

    kp, vp, cp, ks, vs, cs = (jnp.stack(o) for o in outs)
    return (yp.reshape(bsz, seq, D_MODEL), ys.reshape(nseq, t_new, D_MODEL), kp, vp, cp, ks, vs, cs)
```

```python
import functools

import jax
import jax.numpy as jnp
from jax import lax
from jax.experimental import pallas as pl
from jax.experimental.pallas import tpu as pltpu

F32 = jnp.float32
BF16 = jnp.bfloat16

D_MODEL = 1024
ATTN_WIDTH = 512
CONV_WIDTH = 512
HEAD_DIM = 64
N_HEADS = ATTN_WIDTH // HEAD_DIM
ROT_DIM = HEAD_DIM // 4
ROPE_THETA = 500000.0
MOBA_BLOCK = 256
MOBA_TOPK = 3
CONV_K = 31
PAGE_SIZE = 128
NORM_EPS = 1e-6
IN_COLS = 3 * ATTN_WIDTH + 2 * CONV_WIDTH

LANES = 128
SUBLANES = 8
HEADS_PER_SLAB = LANES // HEAD_DIM
MASKED = -1e30
VMEM_LIMIT = 56 * 1024 * 1024

ROW_TILE = 512
FFN_ROW_TILE = 512
SAMPLE_PAGES_PER_CHUNK = 16
SAMPLE_SLOTS = 4
SAMPLE_CONV_SEQS = 8


def _params(*sem):
    return pltpu.CompilerParams(dimension_semantics=sem, vmem_limit_bytes=VMEM_LIMIT)


def _rms(x, g):
    return x * lax.rsqrt(jnp.mean(x * x, axis=-1, keepdims=True) + NORM_EPS) * g


def _rope_tables(pos):
    inv = ROPE_THETA ** (-jnp.arange(0, ROT_DIM, 2, dtype=F32) / ROT_DIM)
    ang = pos.astype(F32)[:, None] * inv[None, :]
    cos, sin = jnp.cos(ang), jnp.sin(ang)
    t, half = pos.shape[0], ROT_DIM // 2
    rest = HEAD_DIM - ROT_DIM
    c = jnp.concatenate([cos, cos, jnp.ones((t, rest), F32)], axis=1)
    lo = jnp.concatenate([-sin, jnp.zeros((t, half + rest), F32)], axis=1)
    hi = jnp.concatenate([jnp.zeros((t, half), F32), sin, jnp.zeros((t, rest), F32)], axis=1)
    return tuple(jnp.tile(a, (1, HEADS_PER_SLAB)) for a in (c, lo, hi))


def _in_proj_kernel(x_ref, g_ref, w_ref, cos_ref, lo_ref, hi_ref, q_ref, k_ref, v_ref, u_ref,
                    *attn_side, channel_major):
    h = _rms(x_ref[...], g_ref[...])
    proj = jnp.dot(h.astype(BF16), w_ref[...], preferred_element_type=F32)
    cos, lo, hi = cos_ref[...], lo_ref[...], hi_ref[...]
    half = ROT_DIM // 2

    def rope(xs):
        return xs * cos + pltpu.roll(xs, LANES - half, 1) * lo + pltpu.roll(xs, half, 1) * hi

    def put(ref, s, val):
        if channel_major:
            ref[0, s * LANES:(s + 1) * LANES, :] = val.T
        else:
            ref[:, s * LANES:(s + 1) * LANES] = val

    a = ATTN_WIDTH
    for s in range(a // LANES):
        put(q_ref, s, rope(proj[:, s * LANES:(s + 1) * LANES]))
        k_rot = rope(proj[:, a + s * LANES:a + (s + 1) * LANES])
        put(k_ref, s, k_rot)
        put(v_ref, s, proj[:, 2 * a + s * LANES:2 * a + (s + 1) * LANES])
        if channel_major:
            ktok_ref, kmean_ref = attn_side
            ktok_ref[:, s * LANES:(s + 1) * LANES] = k_rot.astype(BF16)
            means = [jnp.mean(k_rot[r * MOBA_BLOCK:(r + 1) * MOBA_BLOCK], axis=0, keepdims=True)
                     for r in range(k_rot.shape[0] // MOBA_BLOCK)]
            kmean_ref[0, :, s * LANES:(s + 1) * LANES] = jnp.concatenate(means, axis=0)
    ga = proj[:, 3 * a:3 * a + CONV_WIDTH]
    gb = proj[:, 3 * a + CONV_WIDTH:]
    u_ref[...] = ga * jax.nn.sigmoid(gb)


def _in_proj(x2d, g, w_bf16, tables, tm, seqs=None):
    n = x2d.shape[0]
    table_tiles = tables[0].shape[0] // tm
    tab_spec = pl.BlockSpec((tm, LANES), lambda i: (i % table_tiles, 0))
    row_spec = pl.BlockSpec((tm, ATTN_WIDTH), lambda i: (i, 0))
    row_sds = jax.ShapeDtypeStruct((n, ATTN_WIDTH), F32)
    extra_specs, extra_sds = [], []
    if seqs is None:
        qkv_spec, qkv_sds = row_spec, row_sds
    else:
        tiles_per_seq = n // seqs // tm
        qkv_spec = pl.BlockSpec((1, ATTN_WIDTH, tm),
                                lambda i: (i // tiles_per_seq, 0, i % tiles_per_seq))
        qkv_sds = jax.ShapeDtypeStruct((seqs, ATTN_WIDTH, n // seqs), F32)
        assert tm % MOBA_BLOCK == 0
        blocks = tm // MOBA_BLOCK
        extra_specs = [row_spec, pl.BlockSpec((1, blocks, ATTN_WIDTH), lambda i: (i, 0, 0))]
        extra_sds = [jax.ShapeDtypeStruct((n, ATTN_WIDTH), BF16),
                     jax.ShapeDtypeStruct((n // tm, blocks, ATTN_WIDTH), F32)]
    return pl.pallas_call(
        functools.partial(_in_proj_kernel, channel_major=seqs is not None),
        grid=(n // tm,),
        in_specs=[pl.BlockSpec((tm, D_MODEL), lambda i: (i, 0)),
                  pl.BlockSpec((1, D_MODEL), lambda i: (0, 0)),
                  pl.BlockSpec((D_MODEL, IN_COLS), lambda i: (0, 0)),
                  tab_spec, tab_spec, tab_spec],
        out_specs=[qkv_spec] * 3 + [row_spec] + extra_specs,
        out_shape=[qkv_sds] * 3 + [row_sds] + extra_sds,
        compiler_params=_params("arbitrary"),
        name="in_proj",
    )(x2d, g, w_bf16, *tables)


def _topk_bias(gate, valid, rown, n_rows):
    neg = jnp.finfo(F32).min
    g = jnp.where(valid, gate, neg)
    sel = jnp.zeros(gate.shape, jnp.bool_)
    for _ in range(min(MOBA_TOPK, n_rows)):
        m = jnp.max(g, axis=0, keepdims=True)
        first = jnp.min(jnp.where(g == m, rown, n_rows), axis=0, keepdims=True)
        pick = rown == first
        sel = sel | pick
        g = jnp.where(pick, -jnp.inf, g)
    return jnp.where(sel & valid, 0.0, MASKED)


ATTN_CHAINS = 2
ATTN_CHAIN_BLOCKS = 1
ACC_ROWS = HEAD_DIM + 16
TRIP_LADDER = (8, 4, 2)
LOG2E = 1.4426950408889634


def _attn_prompt_kernel(q_ref, ktok_ref, kmean_ref, vt_in_ref, vo_ref, o_ref,
                        ka_ref, vt_ref, km_ref, qa_ref, sa_ref, sb_ref, *, nb):
    i = pl.program_id(2)
    blk = MOBA_BLOCK
    grp = ATTN_CHAIN_BLOCKS * blk
    step_blocks = ATTN_CHAINS * ATTN_CHAIN_BLOCKS
    heads = range(HEADS_PER_SLAB)
    ones = jnp.ones((ACC_ROWS - HEAD_DIM, grp), BF16)

    @pl.when(i == 0)
    def _():
        lane = lax.broadcasted_iota(jnp.int32, (blk, LANES), 1)
        chan = lax.broadcasted_iota(jnp.int32, (1, LANES), 1) // HEAD_DIM

        def stage_k(j, c):
            rows = pl.ds(pl.multiple_of(j * blk, blk), blk)
            ka_ref[rows, 0:LANES] = ktok_ref[0, rows, :]
            ka_ref[rows, LANES:2 * LANES] = jnp.where(lane == j, 1.0, 0.0).astype(BF16)
            return c
        lax.fori_loop(0, nb, stage_k, 0)
        for h in heads:
            km_ref[h * nb:(h + 1) * nb, :] = jnp.where(chan == h, kmean_ref[0], 0.0)

        def stage_v(g, c):
            cols = pl.ds(pl.multiple_of(g * grp, grp), grp)
            vt = vt_in_ref[0, :, cols].astype(BF16)
            for h in heads:
                vt_ref[g, h, 0:HEAD_DIM, :] = vt[h * HEAD_DIM:(h + 1) * HEAD_DIM, :]
                vt_ref[g, h, HEAD_DIM:ACC_ROWS, :] = ones
            return c
        lax.fori_loop(0, nb // ATTN_CHAIN_BLOCKS, stage_v, 0)

    q_t = q_ref[0]
    gates = jnp.dot(km_ref[...], q_t, precision=lax.Precision.HIGHEST,
                    preferred_element_type=F32)
    k_own = ktok_ref[0, pl.ds(pl.multiple_of(i * blk, blk), blk), :]
    vt_own = vo_ref[0].astype(BF16)
    qs_t = q_t * (LOG2E * HEAD_DIM ** -0.5)
    drow = lax.broadcasted_iota(jnp.int32, q_t.shape, 0) // HEAD_DIM
    qh_t = [jnp.where(drow == h, qs_t, 0.0).astype(BF16) for h in heads]
    s_own = [jnp.dot(k_own, qh_t[h], preferred_element_type=F32) for h in heads]

    rown = lax.broadcasted_iota(jnp.int32, (nb, blk), 0)
    for h in heads:
        bias = _topk_bias(gates[h * nb:(h + 1) * nb], rown < i, rown, nb)
        qa_ref[h, 0:LANES, :] = qh_t[h]
        qa_ref[h, LANES:LANES + nb, :] = bias.astype(BF16)
        if nb < LANES:
            qa_ref[h, LANES + nb:2 * LANES, :] = jnp.zeros((LANES - nb, blk), BF16)

    def score_step(s_ref, step):
        rows = ATTN_CHAINS * grp
        k_rows = ka_ref[pl.ds(pl.multiple_of(step * rows, rows), rows), :]
        for h in heads:
            s = jnp.dot(k_rows, qa_ref[h], preferred_element_type=F32)
            for c in range(ATTN_CHAINS):
                s_ref[c, h] = s[c * grp:(c + 1) * grp]

    def softmax_step(s_ref, step, state):
        out = []
        for h in heads:
            chains = []
            for c in range(ATTN_CHAINS):
                m, acc = state[h][c]
                s = s_ref[c, h]
                m_new = jnp.maximum(m, jnp.max(s, axis=0, keepdims=True))
                p = jnp.exp2(s - m_new).astype(BF16)
                pv = jnp.dot(vt_ref[step * ATTN_CHAINS + c, h], p, preferred_element_type=F32)
                chains.append((m_new, jnp.exp2(m - m_new) * acc + pv))
            out.append(tuple(chains))
        return tuple(out)

    n_steps_all = nb // step_blocks

    def trip(first, n_steps, look_ahead, state):
        for t in range(n_steps):
            ahead, cur = (sb_ref, sa_ref) if t % 2 == 0 else (sa_ref, sb_ref)
            if t + 1 < n_steps or look_ahead:
                score_step(ahead, jnp.minimum(first + t + 1, n_steps_all - 1))
            state = softmax_step(cur, first + t, state)
        return state

    score_step(sa_ref, 0)
    causal = (lax.broadcasted_iota(jnp.int32, (blk, blk), 0)
              <= lax.broadcasted_iota(jnp.int32, (blk, blk), 1))
    init = []
    for h in heads:
        s = jnp.where(causal, s_own[h], MASKED)
        m = jnp.max(s, axis=0, keepdims=True)
        p = jnp.exp2(s - m).astype(BF16)
        vt_ext = jnp.concatenate([vt_own[h * HEAD_DIM:(h + 1) * HEAD_DIM, :], ones[:, :blk]], axis=0)
        acc = jnp.dot(vt_ext, p, preferred_element_type=F32)
        init.append(((m, acc),) + ((m, jnp.zeros_like(acc)),) * (ATTN_CHAINS - 1))
    state, done, left = tuple(init), 0, i
    for k, steps in enumerate(TRIP_LADDER):
        blocks = steps * step_blocks
        if k + 1 < len(TRIP_LADDER):
            spill = blocks - TRIP_LADDER[k + 1] * step_blocks
            n = left // blocks + (left % blocks > spill).astype(jnp.int32)
        else:
            n = (left + blocks - 1) // blocks
        state = lax.fori_loop(
            0, n, lambda u, st, d=done, s=steps, la=k + 1 < len(TRIP_LADDER): trip(d + s * u, s, la, st),
            state)
        done = done + n * steps
        left = jnp.maximum(left - n * blocks, 0)
    outs = []
    for h in heads:
        m_all = functools.reduce(jnp.maximum, [m for m, _ in state[h]])
        acc = sum(jnp.exp2(m - m_all) * a for m, a in state[h])
        outs.append(acc[0:HEAD_DIM] * (1.0 / acc[HEAD_DIM:HEAD_DIM + 1]))
    o_ref[0] = jnp.concatenate(outs, axis=0).T.astype(o_ref.dtype)


def _attn_prompt(q_t, k_tok, k_mean, v_t):
    b, _, s = q_t.shape
    nb = s // MOBA_BLOCK
    assert nb <= LANES and all(steps % 2 == 0 for steps in TRIP_LADDER)
    assert nb % (TRIP_LADDER[0] * ATTN_CHAINS * ATTN_CHAIN_BLOCKS) == 0
    slabs = ATTN_WIDTH // LANES
    grp = ATTN_CHAIN_BLOCKS * MOBA_BLOCK
    all_spec = pl.BlockSpec((1, LANES, s), lambda bi, hp, i: (bi, hp, 0))
    own_spec = pl.BlockSpec((1, LANES, MOBA_BLOCK), lambda bi, hp, i: (bi, hp, i))
    ktok_spec = pl.BlockSpec((1, s, LANES), lambda bi, hp, i: (bi, 0, hp))
    kmean_spec = pl.BlockSpec((1, nb, LANES), lambda bi, hp, i: (bi, 0, hp))
    score_buf = pltpu.VMEM((ATTN_CHAINS, HEADS_PER_SLAB, grp, MOBA_BLOCK), F32)
    return pl.pallas_call(
        functools.partial(_attn_prompt_kernel, nb=nb),
        grid=(b, slabs, nb),
        in_specs=[own_spec, ktok_spec, kmean_spec, all_spec, own_spec],
        out_specs=pl.BlockSpec((1, MOBA_BLOCK, LANES), lambda bi, hp, i: (bi, i, hp)),
        out_shape=jax.ShapeDtypeStruct((b, s, ATTN_WIDTH), BF16),
        scratch_shapes=[pltpu.VMEM((s, 2 * LANES), BF16),
                        pltpu.VMEM((nb // ATTN_CHAIN_BLOCKS, HEADS_PER_SLAB, ACC_ROWS, grp), BF16),
                        pltpu.VMEM((HEADS_PER_SLAB * nb, LANES), F32),
                        pltpu.VMEM((HEADS_PER_SLAB, 2 * LANES, MOBA_BLOCK), BF16),
                        score_buf, score_buf],
        compiler_params=_params("arbitrary", "arbitrary", "arbitrary"),
        name="attn_prompt",
    )(q_t, k_tok, k_mean, v_t, v_t)


def _attn_sample_kernel(pt_ref, q_ref, kn_ref, vn_ref, ck_ref, cv_ref,
                        u_ref, prev_ref, w_ref, bd_ref, g_ref, bl_ref,
                        o_ref, c_ref, tail_ref,
                        buf_ref, s_ref, p_ref, own_ref, ext_ref, sem,
                        *, n_pages, ppc, slots, nseq, t_new, conv_tiles_per_seq):
    b = pl.program_id(0)
    conv_i = b % conv_tiles_per_seq
    conv_rows = u_ref.shape[1]
    conv_chunks = conv_rows // CONV_CHUNK
    _conv_fill(ext_ref, u_ref[0], prev_ref[0], conv_i == 0)
    blk = MOBA_BLOCK
    nck = n_pages // ppc
    toks = ppc * PAGE_SIZE
    nbk = n_pages * PAGE_SIZE // blk
    nchunks = 2 * nck
    nq = t_new * N_HEADS
    nt = (((1,), (1,)), ((), ()))

    def copies(bb, c, slot):
        src = ck_ref if c < nck else cv_ref
        base = (c % nck) * ppc
        return [pltpu.make_async_copy(src.at[pt_ref[bb, base + p]],
                                      buf_ref.at[slot, :, pl.ds(p * PAGE_SIZE, PAGE_SIZE)],
                                      sem.at[slot]) for p in range(ppc)]

    ahead = slots - 1

    @pl.when(b == 0)
    def _():
        for c in range(ahead):
            for cp in copies(b, c, c % slots):
                cp.start()

    q = q_ref[0]
    col_head = lax.broadcasted_iota(jnp.int32, (N_HEADS, ATTN_WIDTH), 1) // HEAD_DIM
    row_head = lax.broadcasted_iota(jnp.int32, (N_HEADS, ATTN_WIDTH), 0)
    headmask = col_head == row_head
    qexp = jnp.concatenate([jnp.where(headmask, q[t:t + 1, :] * HEAD_DIM ** -0.5, 0.0)
                            for t in range(t_new)], axis=0).astype(BF16)

    acc = m = l = None
    for c in range(nchunks):
        slot = c % slots
        for cp in copies(b, c, slot):
            cp.wait()
        nxt = c + ahead
        if nxt < nchunks:
            for cp in copies(b, nxt, nxt % slots):
                cp.start()
        else:
            @pl.when(b + 1 < nseq)
            def _():
                for cp in copies(b + 1, nxt - nchunks, nxt % slots):
                    cp.start()

        if c < nck:
            s_ref[:, c * toks:(c + 1) * toks] = jnp.dot(qexp, buf_ref[slot].astype(BF16),
                                                        preferred_element_type=F32)
        if c == nck - 1:
            gate = [jnp.sum(s_ref[:, n * blk:(n + 1) * blk], axis=1, keepdims=True)
                    for n in range(nbk)]
            picked = [jnp.zeros((nq, 1), jnp.bool_)] * nbk
            for _ in range(min(MOBA_TOPK, nbk)):
                top = functools.reduce(jnp.maximum, gate)
                found = jnp.zeros((nq, 1), jnp.bool_)
                for n in range(nbk):
                    hit = (gate[n] == top) & jnp.logical_not(found)
                    found = found | hit
                    picked[n] = picked[n] | hit
                    gate[n] = jnp.where(hit, -jnp.inf, gate[n])
            bias = [jnp.where(pk, 0.0, MASKED) for pk in picked]

            own_ref[...] = jnp.zeros(own_ref.shape, F32)
            own_ref[0:t_new, :] = kn_ref[0]
            sn = lax.dot_general(qexp, own_ref[...].astype(BF16), nt,
                                 preferred_element_type=F32)
            key_t = lax.broadcasted_iota(jnp.int32, sn.shape, 1)
            qry_t = lax.broadcasted_iota(jnp.int32, sn.shape, 0) // N_HEADS
            sn = jnp.where((key_t <= qry_t) & (key_t < t_new), sn, MASKED)
            m = jnp.max(sn, axis=1, keepdims=True)
            for n in range(nbk):
                blk_max = jnp.max(s_ref[:, n * blk:(n + 1) * blk], axis=1, keepdims=True)
                m = jnp.maximum(m, blk_max + bias[n])
            pn = jnp.exp(sn - m)
            l = jnp.sum(pn, axis=1, keepdims=True)
            for n in range(nbk):
                p = jnp.exp(s_ref[:, n * blk:(n + 1) * blk] + (bias[n] - m))
                l = l + jnp.sum(p, axis=1, keepdims=True)
                p_ref[:, n * blk:(n + 1) * blk] = p.astype(BF16)
            own_ref[0:t_new, :] = vn_ref[0]
            acc = jnp.dot(pn.astype(BF16), own_ref[...].astype(BF16),
                          preferred_element_type=F32)
        if c >= nck:
            acc = acc + lax.dot_general(p_ref[:, (c - nck) * toks:(c - nck + 1) * toks],
                                        buf_ref[slot].astype(BF16), nt,
                                        preferred_element_type=F32)
        for cc in range(c * conv_chunks // nchunks, (c + 1) * conv_chunks // nchunks):
            _conv_chunk(ext_ref, cc, w_ref, bd_ref, g_ref, bl_ref, c_ref)

    @pl.when(conv_i == conv_tiles_per_seq - 1)
    def _():
        tail_ref[0] = _conv_tail(ext_ref, conv_rows)

    out = acc * (1.0 / l)
    rows = [jnp.sum(jnp.where(headmask, out[t * N_HEADS:(t + 1) * N_HEADS, :], 0.0),
                    axis=0, keepdims=True) for t in range(t_new)]
    o_ref[0] = jnp.concatenate(rows, axis=0).astype(o_ref.dtype)


def _attn_sample_conv_prompt(q, k_new, v_new, cache_kt, cache_vt, page_table,
                             u, w_dw, b_dw, g_ln, b_ln):
    nseq, t_new, _ = q.shape
    n_pages = page_table.shape[1]
    ppc = min(SAMPLE_PAGES_PER_CHUNK, n_pages)
    past = n_pages * PAGE_SIZE
    assert past % MOBA_BLOCK == 0 and n_pages % ppc == 0
    assert t_new <= LANES
    nchunks = 2 * (n_pages // ppc)
    slots = min(SAMPLE_SLOTS, nchunks)
    assert nchunks % slots == 0
    bsz, seq, _ = u.shape
    assert (bsz * seq) % nseq == 0
    tc = bsz * seq // nseq
    assert seq % tc == 0 and tc % CONV_CHUNK == 0 and tc % CONV_HALO == 0
    tps = seq // tc
    halo_per_tile = tc // CONV_HALO

    tok_spec = pl.BlockSpec((1, t_new, ATTN_WIDTH), lambda b, pt: (b, 0, 0))
    any_spec = pl.BlockSpec(memory_space=pl.ANY)
    tile_spec = pl.BlockSpec((1, tc, CONV_WIDTH), lambda b, pt: (b // tps, b % tps, 0))
    prev_spec = pl.BlockSpec(
        (1, CONV_HALO, CONV_WIDTH),
        lambda b, pt: (b // tps, jnp.maximum((b % tps) * halo_per_tile - 1, 0), 0))
    vec = pl.BlockSpec((1, CONV_WIDTH), lambda b, pt: (0, 0))
    nq = t_new * N_HEADS
    return pl.pallas_call(
        functools.partial(_attn_sample_kernel, n_pages=n_pages, ppc=ppc, slots=slots, nseq=nseq,
                          t_new=t_new, conv_tiles_per_seq=tps),
        grid_spec=pltpu.PrefetchScalarGridSpec(
            num_scalar_prefetch=1,
            grid=(nseq,),
            in_specs=[tok_spec, tok_spec, tok_spec, any_spec, any_spec,
                      tile_spec, prev_spec,
                      pl.BlockSpec((CONV_K, CONV_WIDTH), lambda b, pt: (0, 0)), vec, vec, vec],
            out_specs=[tok_spec, tile_spec,
                       pl.BlockSpec((1, CONV_K - 1, CONV_WIDTH), lambda b, pt: (b // tps, 0, 0))],
            scratch_shapes=[pltpu.VMEM((slots, ATTN_WIDTH, ppc * PAGE_SIZE), F32),
                            pltpu.VMEM((nq, past), F32),
                            pltpu.VMEM((nq, past), BF16),
                            pltpu.VMEM((LANES, ATTN_WIDTH), F32),
                            pltpu.VMEM((SUBLANES, CONV_HALO + tc, CONV_WIDTH), F32),
                            pltpu.SemaphoreType.DMA((slots,))]),
        out_shape=[jax.ShapeDtypeStruct((nseq, t_new, ATTN_WIDTH), BF16),
                   jax.ShapeDtypeStruct((bsz, seq, CONV_WIDTH), BF16),
                   jax.ShapeDtypeStruct((bsz, CONV_K - 1, CONV_WIDTH), F32)],
        compiler_params=_params("arbitrary"),
        name="attn_sample_conv_prompt",
    )(page_table, q, k_new, v_new, cache_kt, cache_vt, u, u, w_dw, b_dw, g_ln, b_ln)


def _ln_swish(y, g, b):
    mu = jnp.mean(y, axis=-1, keepdims=True)
    yc = y - mu
    yn = yc * lax.rsqrt(jnp.mean(yc * yc, axis=-1, keepdims=True) + NORM_EPS) * g + b
    return yn * jax.nn.sigmoid(yn)


CONV_HALO = 32
CONV_CHUNK = 64


def _conv_fill(ext_ref, tile, prev, first_tile):
    rows = tile.shape[0]
    ext_ref[0, 0:CONV_HALO, :] = jnp.where(first_tile, 0.0, prev)
    ext_ref[0, CONV_HALO:CONV_HALO + rows, :] = tile
    shifted_rows = CONV_HALO + rows - SUBLANES
    for r in range(1, SUBLANES):
        ext_ref[r, 0:shifted_rows, :] = ext_ref[0, r:r + shifted_rows, :]


def _conv_chunk(ext_ref, c, w_ref, bd_ref, g_ref, bl_ref, c_ref):
    first = CONV_HALO - (CONV_K - 1)
    acc = jnp.zeros((CONV_CHUNK, CONV_WIDTH), F32)
    for k in range(CONV_K):
        r = (first + k) % SUBLANES
        start = first + k - r + c * CONV_CHUNK
        acc = acc + ext_ref[r, start:start + CONV_CHUNK, :] * w_ref[k:k + 1, :]
    y = _ln_swish(acc + bd_ref[...], g_ref[...], bl_ref[...])
    c_ref[0, c * CONV_CHUNK:(c + 1) * CONV_CHUNK, :] = y.astype(c_ref.dtype)


def _conv_tail(ext_ref, rows):
    return ext_ref[0, CONV_HALO + rows - (CONV_K - 1):CONV_HALO + rows, :]


def _conv_sample_kernel(u_ref, hist_ref, w_ref, bd_ref, g_ref, bl_ref, c_ref, tail_ref, ext_ref,
                        *, t_new):
    hist = CONV_K - 1
    ext_ref[:, 0:hist, :] = hist_ref[...]
    ext_ref[:, hist:hist + t_new, :] = u_ref[...]
    w = w_ref[...]
    for t in range(t_new):
        y = jnp.sum(ext_ref[:, t:t + CONV_K, :] * w[None], axis=1) + bd_ref[...]
        c_ref[:, t, :] = _ln_swish(y, g_ref[...], bl_ref[...]).astype(c_ref.dtype)
    tail_ref[...] = ext_ref[:, t_new:t_new + hist, :]


def _conv_sample(u, hist, w_dw, b_dw, g_ln, b_ln):
    nseq, t_new, _ = u.shape
    ns = min(SAMPLE_CONV_SEQS, nseq)
    vec = pl.BlockSpec((1, CONV_WIDTH), lambda i: (0, 0))
    tok = pl.BlockSpec((ns, t_new, CONV_WIDTH), lambda i: (i, 0, 0))
    his = pl.BlockSpec((ns, CONV_K - 1, CONV_WIDTH), lambda i: (i, 0, 0))
    return pl.pallas_call(
        functools.partial(_conv_sample_kernel, t_new=t_new),
        grid=(nseq // ns,),
        in_specs=[tok, his, pl.BlockSpec((CONV_K, CONV_WIDTH), lambda i: (0, 0)), vec, vec, vec],
        out_specs=[tok, his],
        out_shape=[jax.ShapeDtypeStruct((nseq, t_new, CONV_WIDTH), BF16),
                   jax.ShapeDtypeStruct((nseq, CONV_K - 1, CONV_WIDTH), F32)],
        scratch_shapes=[pltpu.VMEM((ns, CONV_K - 1 + t_new, CONV_WIDTH), F32)],
        compiler_params=_params("arbitrary"),
        name="conv_sample",
    )(u, hist, w_dw, b_dw, g_ln, b_ln)


def _ffn_kernel(x_ref, a_ref, c_ref, woa_ref, woc_ref, g2_ref, wg_ref, wu_ref, wd_ref, gf_ref,
                y_ref, *, final_norm):
    mix = jnp.dot(a_ref[...], woa_ref[...], preferred_element_type=F32)
    mix = mix + jnp.dot(c_ref[...], woc_ref[...], preferred_element_type=F32)
    x1 = x_ref[...] + mix
    h2 = _rms(x1, g2_ref[...]).astype(BF16)
    gate = jnp.dot(h2, wg_ref[...], preferred_element_type=F32)
    up = jnp.dot(h2, wu_ref[...], preferred_element_type=F32)
    ff = (gate * jax.nn.sigmoid(gate) * up).astype(BF16)
    x2 = x1 + jnp.dot(ff, wd_ref[...], preferred_element_type=F32)
    y_ref[...] = _rms(x2, gf_ref[...]) if final_norm else x2


def _ffn(x2d, a2d, c2d, woa, woc, g2, wg, wu, wd, gf, tm, final_norm):
    n = x2d.shape[0]
    d_ff = wg.shape[1]

    def resident(shape):
        return pl.BlockSpec(shape, lambda i: (0, 0), pipeline_mode=pl.Buffered(1))

    return pl.pallas_call(
        functools.partial(_ffn_kernel, final_norm=final_norm),
        grid=(n // tm,),
        in_specs=[pl.BlockSpec((tm, D_MODEL), lambda i: (i, 0)),
                  pl.BlockSpec((tm, ATTN_WIDTH), lambda i: (i, 0)),
                  pl.BlockSpec((tm, CONV_WIDTH), lambda i: (i, 0)),
                  resident((ATTN_WIDTH, D_MODEL)), resident((CONV_WIDTH, D_MODEL)),
                  resident((1, D_MODEL)),
                  resident((D_MODEL, d_ff)), resident((D_MODEL, d_ff)), resident((d_ff, D_MODEL)),
                  resident((1, D_MODEL))],
        out_specs=pl.BlockSpec((tm, D_MODEL), lambda i: (i, 0)),
        out_shape=jax.ShapeDtypeStruct((n, D_MODEL), F32),
        compiler_params=_params("arbitrary"),
        name="out_proj_ffn",
    )(x2d, a2d, c2d, woa, woc, g2, wg, wu, wd, gf)


def kernel(x_prompt, x_sample, cache_k, cache_v, state_conv, page_table, g_mix_norm, w_in, w_dw,
           b_dw, g_conv_ln, b_conv_ln, w_out, g_ffn_norm, w_gate, w_up, w_down, g_final):
    depth = w_in.shape[0]
    bsz, seq, _ = x_prompt.shape
    nseq, t_new, _ = x_sample.shape
    n_phys = cache_k.shape[1]
    past = page_table.shape[1] * PAGE_SIZE
    assert seq % MOBA_BLOCK == 0 and seq % ROW_TILE == 0

    def pages_t(cache):
        return jnp.transpose(cache, (0, 1, 3, 4, 2)).reshape(depth * n_phys, ATTN_WIDTH, PAGE_SIZE)
    cache_kt, cache_vt = pages_t(cache_k), pages_t(cache_v)

    tabs_p = _rope_tables(jnp.arange(seq, dtype=jnp.int32))
    pos_s = past + jnp.arange(t_new, dtype=jnp.int32)
    tabs_s = _rope_tables(jnp.tile(pos_s, nseq))
    n_p, n_s = bsz * seq, nseq * t_new
    tm_s = min(ROW_TILE, n_s)
    tf_s = min(FFN_ROW_TILE, n_s)

    yp = x_prompt.reshape(n_p, D_MODEL)
    ys = x_sample.reshape(n_s, D_MODEL)
    row = lambda a: a.reshape(1, -1)
    outs = [[] for _ in range(6)]
    for l in range(depth):
        last = l == depth - 1
        w_in_b = w_in[l].astype(BF16)
        woa, woc = w_out[l, :ATTN_WIDTH].astype(BF16), w_out[l, ATTN_WIDTH:].astype(BF16)
        wg, wu, wd = w_gate[l].astype(BF16), w_up[l].astype(BF16), w_down[l].astype(BF16)
        conv_w = (w_dw[l], row(b_dw[l]), row(g_conv_ln[l]), row(b_conv_ln[l]))
        ffn_w = (woa, woc, row(g_ffn_norm[l]), wg, wu, wd, row(g_final))

        q_t, k_t, v_t, u_p, k_tok, k_mean = _in_proj(yp, row(g_mix_norm[l]), w_in_b, tabs_p,
                                                     ROW_TILE, seqs=bsz)
        q, k, v, u_s = _in_proj(ys, row(g_mix_norm[l]), w_in_b, tabs_s, tm_s)
        a_p = _attn_prompt(q_t, k_tok.reshape(bsz, seq, ATTN_WIDTH),
                           k_mean.reshape(bsz, seq // MOBA_BLOCK, ATTN_WIDTH), v_t)
        shp = (nseq, t_new, ATTN_WIDTH)
        a_s, c_p, tail = _attn_sample_conv_prompt(
            q.reshape(shp), k.reshape(shp), v.reshape(shp), cache_kt, cache_vt,
            page_table + l * n_phys, u_p.reshape(bsz, seq, CONV_WIDTH), *conv_w)
        yp = _ffn(yp, a_p.reshape(n_p, ATTN_WIDTH), c_p.reshape(n_p, CONV_WIDTH), *ffn_w,
                  FFN_ROW_TILE, last)
        token_major = lambda t: jnp.transpose(t.reshape(bsz, N_HEADS, HEAD_DIM, seq), (0, 3, 1, 2))
        outs[0].append(token_major(k_t))
        outs[1].append(token_major(v_t))
        outs[2].append(tail)

        c_s, tail = _conv_sample(u_s.reshape(nseq, t_new, CONV_WIDTH), state_conv[l], *conv_w)
        ys = _ffn(ys, a_s.reshape(n_s, ATTN_WIDTH), c_s.reshape(n_s, CONV_WIDTH), *ffn_w, tf_s,
                  last)
        outs[3].append(k.reshape(nseq, t_new, N_HEADS, HEAD_DIM))
        outs[4].append(v.reshape(nseq, t_new, N_HEADS, HEAD_DIM))
        outs[5].append(tail)

    kp, vp, cp, ks, vs, cs = (jnp.stack(o) for o in outs)
    return (yp.reshape(bsz, seq, D_MODEL), ys.reshape(nseq, t_new, D_MODEL), kp, vp, cp, ks, vs, cs)
```
